```python
import jax, jax.numpy as jnp
from jax import lax
import numpy as np

D_MODEL = 1024
BATCH = 32
SEQ = 256
DEPTH = 2
DEC_BATCH = 2
DEC_SEQ = 2048
PAST_LEN = 512

GRID_W = 64
N_EVEN = (DEPTH + 1) // 2
N_ODD = DEPTH // 2
N_MOD = 6
RMS_EPS = 1e-6
ROPE_THETA = 10000.0
NEG_INF = -1e30
Q_BLOCK = 128

CONV_DIM = 512
CONV_W = 3
HEAD_DIM_B = 64
N_HEADS_B = 8
N_KV_B = 2
WINDOW = 128
BAND_BLOCK = 128
EVEN_IN = 3 * CONV_DIM + (N_HEADS_B + 2 * N_KV_B) * HEAD_DIM_B
EVEN_OUT = CONV_DIM + N_HEADS_B * HEAD_DIM_B
N_HEADS_C = 8
Q_LORA = 384
KV_LORA = 256
NOPE_C = 64
ROPE_C = 32
V_DIM_C = 64
MLA_SCALE = (NOPE_C + ROPE_C) ** -0.5
CHUNK = 128
N_GROUPS_D = 4
GROUP_DIM_D = 128
D_D = N_GROUPS_D * GROUP_DIM_D
ODD_IN = Q_LORA + KV_LORA + ROPE_C + 2 * D_D
ODD_OUT = N_HEADS_C * V_DIM_C + D_D
PEER_HEADS = 8
PEER_DK = 256
N_KEYS = 128
N_EXPERTS = N_KEYS * N_KEYS
PEER_TOPK = 16
PEER_BLOCK = 128

kernel_name = "hybrid_diffusion_prefix_step"


def rmsnorm(x, g):
    xf = x.astype(jnp.float32)
    y = xf * lax.rsqrt(jnp.mean(xf * xf, axis=-1, keepdims=True) + RMS_EPS)
    return (y * g.astype(jnp.float32)).astype(x.dtype)


def modulation(cvec, w_mod, b_mod):
    m = jax.nn.silu(cvec) @ w_mod + b_mod
    return m.reshape(cvec.shape[0], N_MOD, D_MODEL)


def adaln(x, g, shift, scale):
    return rmsnorm(x, g) * (1.0 + scale[:, None, :]) + shift[:, None, :]


def axial_rope(x):
    s, dr = x.shape[1], x.shape[-1]
    rows = s // GRID_W
    t = jnp.arange(rows * GRID_W)
    row = (t // GRID_W).astype(jnp.float32)
    col = (t % GRID_W).astype(jnp.float32)
    n_freq = dr // 4
    freqs = jnp.power(ROPE_THETA, -jnp.arange(n_freq, dtype=jnp.float32) / n_freq)
    ang = jnp.concatenate([row[:, None] * freqs, col[:, None] * freqs], axis=-1)[:, None, :]
    cos, sin = jnp.cos(ang), jnp.sin(ang)
    xf = x.astype(jnp.float32)
    x1, x2 = xf[..., : dr // 2], xf[..., dr // 2:]
    return jnp.concatenate([x1 * cos - x2 * sin, x1 * sin + x2 * cos], axis=-1).astype(x.dtype)


def short_conv3(u, w):
    up = jnp.pad(u, ((0, 0), (1, 1), (0, 0)))
    return up[:, :-2] * w[0] + up[:, 1:-1] * w[1] + up[:, 2:] * w[2]


def attend_blocked(q, k, v, scale, sink=None):
    b, sq, h, dq = q.shape
    hkv = k.shape[2]
    g = h // hkv
    dv = v.shape[-1]
    nb = sq // Q_BLOCK
    qb = jnp.moveaxis(q.reshape(b, nb, Q_BLOCK, hkv, g, dq), 1, 0)

    def block(qblk):
        s = jnp.einsum('bqhgd,bkhd->bhgqk', qblk, k).astype(jnp.float32) * scale
        if sink is not None:
            sk = jnp.broadcast_to(sink.astype(jnp.float32).reshape(1, hkv, g, 1, 1), s.shape[:-1] + (1,))
            p = jax.nn.softmax(jnp.concatenate([s, sk], axis=-1), axis=-1)[..., :-1]
        else:
            p = jax.nn.softmax(s, axis=-1)
        return jnp.einsum('bhgqk,bkhd->bqhgd', p.astype(v.dtype), v)

    o = lax.map(block, qb)
    return jnp.moveaxis(o, 0, 1).reshape(b, sq, h, dv)


def window_attn_ctx(q, k, v, kc, vc, sink, scale):
    b, s, h, d = q.shape
    hkv = k.shape[2]
    g = h // hkv
    nb = s // BAND_BLOCK
    sc = kc.shape[1]
    qb = q.reshape(b, nb, BAND_BLOCK, hkv, g, d)

    def neighbours(x):
        xr = x.reshape(b, nb, BAND_BLOCK, hkv, x.shape[-1])
        xp = jnp.pad(xr, ((0, 0), (1, 1), (0, 0), (0, 0), (0, 0)))
        return jnp.concatenate([xp[:, :-2], xp[:, 1:-1], xp[:, 2:]], axis=2)

    kw, vw = neighbours(k), neighbours(v)
    s_loc = jnp.einsum('bnqhgd,bnkhd->bnhgqk', qb, kw).astype(jnp.float32) * scale
    qi = jnp.arange(BAND_BLOCK)[:, None]
    kj = jnp.arange(3 * BAND_BLOCK)[None, :]
    in_window = jnp.abs(kj - BAND_BLOCK - qi) <= WINDOW
    key_block = jnp.arange(nb)[:, None, None] + (kj // BAND_BLOCK - 1)[None]
    valid = in_window[None] & (key_block >= 0) & (key_block < nb)
    s_loc = jnp.where(valid[None, :, None, None], s_loc, NEG_INF)
    s_ctx = jnp.einsum('bnqhgd,bkhd->bnhgqk', qb, kc).astype(jnp.float32) * scale
    sk = jnp.broadcast_to(sink.astype(jnp.float32).reshape(1, 1, hkv, g, 1, 1), s_ctx.shape[:-1] + (1,))
    p = jax.nn.softmax(jnp.concatenate([s_ctx, s_loc, sk], axis=-1), axis=-1)
    p_ctx = p[..., :sc].astype(v.dtype)
    p_loc = p[..., sc:sc + 3 * BAND_BLOCK].astype(v.dtype)
    o = (jnp.einsum('bnhgqk,bkhd->bnqhgd', p_ctx, vc)
         + jnp.einsum('bnhgqk,bnkhd->bnqhgd', p_loc, vw))
    return o.reshape(b, s, h, d)


def even_project(h, w_in, conv_w):
    b, s, _ = h.shape
    nq = N_HEADS_B * HEAD_DIM_B
    nkv = N_KV_B * HEAD_DIM_B
    z = h @ w_in
    gb, gc, xa, q, k, v = jnp.split(
        z, [CONV_DIM, 2 * CONV_DIM, 3 * CONV_DIM, 3 * CONV_DIM + nq, 3 * CONV_DIM + nq + nkv], axis=-1)
    y_a = gb * short_conv3(gc * xa, conv_w)
    q = q.reshape(b, s, N_HEADS_B, HEAD_DIM_B)
    k = k.reshape(b, s, N_KV_B, HEAD_DIM_B)
    v = v.reshape(b, s, N_KV_B, HEAD_DIM_B)
    return y_a, q, k, v


def even_mixer_ctx(h, w_in, conv_w, sink, w_out):
    b, s, _ = h.shape
    y_a, q, k, v = even_project(h, w_in, conv_w)
    y_b = attend_blocked(q, k, v, HEAD_DIM_B ** -0.5, sink)
    out = jnp.concatenate([y_a, y_b.reshape(b, s, -1)], axis=-1) @ w_out
    return out, k, v


def even_mixer_lat(h, cache_k, cache_v, w_in, conv_w, sink, w_out):
    b, s, _ = h.shape
    y_a, q, k, v = even_project(h, w_in, conv_w)
    y_b = window_attn_ctx(axial_rope(q), axial_rope(k), v, cache_k, cache_v, sink, HEAD_DIM_B ** -0.5)
    return jnp.concatenate([y_a, y_b.reshape(b, s, -1)], axis=-1) @ w_out


def chunk_gmlp(dz, g_sgu, w_s, b_s):
    a = jax.nn.gelu(dz)
    u, vv = a[..., :D_D], a[..., D_D:]
    vv = rmsnorm(vv, g_sgu)
    b, s, _ = vv.shape
    vr = vv.reshape(b, s // CHUNK, CHUNK, N_GROUPS_D, GROUP_DIM_D)
    mixed = jnp.einsum('gpq,bnqgc->bnpgc', w_s, vr) + b_s.T[:, :, None]
    return u * mixed.reshape(b, s, D_D)


def odd_project(h, w_in, g_q, w_uq, g_kv, g_sgu, w_s, b_s):
    b, s, _ = h.shape
    z = h @ w_in
    cq, ckv, kr, dz = jnp.split(z, [Q_LORA, Q_LORA + KV_LORA, Q_LORA + KV_LORA + ROPE_C], axis=-1)
    q = (rmsnorm(cq, g_q) @ w_uq).reshape(b, s, N_HEADS_C, NOPE_C + ROPE_C)
    ckv_n = rmsnorm(ckv, g_kv)
    y_d = chunk_gmlp(dz, g_sgu, w_s, b_s)
    return q, ckv_n, kr, y_d


def mla_expand(ckv_n, w_ukv):
    b, s, _ = ckv_n.shape
    kv = (ckv_n @ w_ukv).reshape(b, s, N_HEADS_C, NOPE_C + V_DIM_C)
    return kv[..., :NOPE_C], kv[..., NOPE_C:]


def mla_keys(k_nope, k_rope):
    kr = jnp.broadcast_to(k_rope[:, :, None, :], k_nope.shape[:-1] + (ROPE_C,))
    return jnp.concatenate([k_nope, kr], axis=-1)


def odd_mixer_ctx(h, w_in, g_q, w_uq, g_kv, w_ukv, g_sgu, w_s, b_s, w_out):
    b, s, _ = h.shape
    q, ckv_n, kr, y_d = odd_project(h, w_in, g_q, w_uq, g_kv, g_sgu, w_s, b_s)
    k_nope, v = mla_expand(ckv_n, w_ukv)
    y_c = attend_blocked(q, mla_keys(k_nope, kr), v, MLA_SCALE)
    out = jnp.concatenate([y_c.reshape(b, s, -1), y_d], axis=-1) @ w_out
    return out, ckv_n, kr


def odd_mixer_lat(h, cache_ckv, cache_krope, w_in, g_q, w_uq, g_kv, w_ukv, g_sgu, w_s, b_s, w_out):
    b, s, _ = h.shape
    q, ckv_n, kr, y_d = odd_project(h, w_in, g_q, w_uq, g_kv, g_sgu, w_s, b_s)
    q = jnp.concatenate([q[..., :NOPE_C], axial_rope(q[..., NOPE_C:])], axis=-1)
    kr_lat = axial_rope(kr[:, :, None, :])[:, :, 0]
    k_nope_l, v_l = mla_expand(ckv_n, w_ukv)
    k_nope_c, v_c = mla_expand(cache_ckv, w_ukv)
    k_all = jnp.concatenate([mla_keys(k_nope_c, cache_krope), mla_keys(k_nope_l, kr_lat)], axis=1)
    v_all = jnp.concatenate([v_c, v_l], axis=1)
    y_c = attend_blocked(q, k_all, v_all, MLA_SCALE)
    return jnp.concatenate([y_c.reshape(b, s, -1), y_d], axis=-1) @ w_out


def peer(h, wq, subkeys, u_tab, v_tab):
    b, s, d = h.shape
    xb_all = h.reshape((b * s) // PEER_BLOCK, PEER_BLOCK, d)

    def block(xb):
        tb = xb.shape[0]
        q = (xb @ wq).reshape(tb, PEER_HEADS, 2, PEER_DK // 2)
        sc = jnp.einsum('thsd,hskd->thsk', q, subkeys).astype(jnp.float32)
        sv, si = lax.top_k(sc, PEER_TOPK)
        cand = (sv[:, :, 0, :, None] + sv[:, :, 1, None, :]).reshape(tb, PEER_HEADS, PEER_TOPK * PEER_TOPK)
        best, ci = lax.top_k(cand, PEER_TOPK)
        i1 = jnp.take_along_axis(si[:, :, 0], ci // PEER_TOPK, axis=-1)
        i2 = jnp.take_along_axis(si[:, :, 1], ci % PEER_TOPK, axis=-1)
        idx = i1 * N_KEYS + i2
        gsm = jax.nn.softmax(best, axis=-1)
        ue = u_tab[idx]
        act = jax.nn.gelu(jnp.einsum('td,thkd->thk', xb, ue))
        ve = v_tab[idx]
        return jnp.einsum('thk,thkd->td', (gsm * act).astype(ve.dtype), ve)

    return lax.map(block, xb_all).reshape(b, s, d)


def setup_inputs(seed: int = 0) -> dict:
    key = jax.random.key(seed)
    ks = iter(jax.random.split(key, 32))

    def nrm(shape, scale=1.0):
        return scale * jax.random.normal(next(ks), shape, jnp.float32)

    def gain(shape):
        return 1.0 + 0.05 * jax.random.normal(next(ks), shape, jnp.float32)

    D = D_MODEL
    return {
        'x_prompt': nrm((BATCH, SEQ, D)),
        'x_sample': nrm((DEC_BATCH, DEC_SEQ, D)),
        'cache_attn_k': nrm((DEC_BATCH, N_EVEN, PAST_LEN, N_KV_B, HEAD_DIM_B)),
        'cache_attn_v': nrm((DEC_BATCH, N_EVEN, PAST_LEN, N_KV_B, HEAD_DIM_B)),
        'cache_mla_ckv': nrm((DEC_BATCH, N_ODD, PAST_LEN, KV_LORA)),
        'cache_mla_krope': nrm((DEC_BATCH, N_ODD, PAST_LEN, ROPE_C)),
        'c': nrm((DEC_BATCH, D)),
        'c_ctx': nrm((D,)),
        'w_mod': nrm((DEPTH, D, N_MOD * D), 0.5 * D ** -0.5),
        'b_mod': nrm((DEPTH, N_MOD * D), 0.02),
        'g_norm1': gain((DEPTH, D)),
        'g_norm2': gain((DEPTH, D)),
        'ev_w_in': nrm((N_EVEN, D, EVEN_IN), D ** -0.5),
        'ev_conv_w': nrm((N_EVEN, CONV_W, CONV_DIM), CONV_W ** -0.5),
        'ev_sink': nrm((N_EVEN, N_HEADS_B), 0.5),
        'ev_w_out': nrm((N_EVEN, EVEN_OUT, D), EVEN_OUT ** -0.5),
        'od_w_in': nrm((N_ODD, D, ODD_IN), D ** -0.5),
        'od_g_qnorm': gain((N_ODD, Q_LORA)),
        'od_w_uq': nrm((N_ODD, Q_LORA, N_HEADS_C * (NOPE_C + ROPE_C)), Q_LORA ** -0.5),
        'od_g_kvnorm': gain((N_ODD, KV_LORA)),
        'od_w_ukv': nrm((N_ODD, KV_LORA, N_HEADS_C * (NOPE_C + V_DIM_C)), KV_LORA ** -0.5),
        'od_g_sgu': gain((N_ODD, D_D)),
        'od_w_spatial': nrm((N_ODD, N_GROUPS_D, CHUNK, CHUNK), 0.5 * CHUNK ** -0.5),
        'od_b_spatial': gain((N_ODD, N_GROUPS_D, CHUNK)),
        'od_w_out': nrm((N_ODD, ODD_OUT, D), ODD_OUT ** -0.5),
        'peer_wq': nrm((DEPTH, D, PEER_HEADS * PEER_DK), D ** -0.5),
        'peer_subkeys': nrm((DEPTH, PEER_HEADS, 2, N_KEYS, PEER_DK // 2), (PEER_DK // 2) ** -0.5),
        'peer_u': nrm((DEPTH, N_EXPERTS, D), D ** -0.5),
        'peer_v': nrm((DEPTH, N_EXPERTS, D), 1.0),
        'g_final': gain((D,)),
    }


def reference(x_prompt, x_sample, cache_attn_k, cache_attn_v, cache_mla_ckv, cache_mla_krope, c, c_ctx,
              w_mod, b_mod, g_norm1, g_norm2, ev_w_in, ev_conv_w, ev_sink, ev_w_out,
              od_w_in, od_g_qnorm, od_w_uq, od_g_kvnorm, od_w_ukv, od_g_sgu, od_w_spatial, od_b_spatial,
              od_w_out, peer_wq, peer_subkeys, peer_u, peer_v, g_final):
    xp, xs = x_prompt, x_sample
    new_k, new_v, new_ckv, new_kr = [], [], [], []
    for layer in range(DEPTH):
        j = layer // 2
        mp = modulation(c_ctx[None, :], w_mod[layer], b_mod[layer])
        ms = modulation(c, w_mod[layer], b_mod[layer])
        hp = adaln(xp, g_norm1[layer], mp[:, 0], mp[:, 1])
        hs = adaln(xs, g_norm1[layer], ms[:, 0], ms[:, 1])
        if layer % 2 == 0:
            op, kp, vp = even_mixer_ctx(hp, ev_w_in[j], ev_conv_w[j], ev_sink[j], ev_w_out[j])
            os_ = even_mixer_lat(hs, cache_attn_k[:, j], cache_attn_v[:, j],
                                 ev_w_in[j], ev_conv_w[j], ev_sink[j], ev_w_out[j])
            new_k.append(kp)
            new_v.append(vp)
        else:
            op, ckv_p, kr_p = odd_mixer_ctx(hp, od_w_in[j], od_g_qnorm[j], od_w_uq[j], od_g_kvnorm[j],
                                            od_w_ukv[j], od_g_sgu[j], od_w_spatial[j], od_b_spatial[j],
                                            od_w_out[j])
            os_ = odd_mixer_lat(hs, cache_mla_ckv[:, j], cache_mla_krope[:, j],
                                od_w_in[j], od_g_qnorm[j], od_w_uq[j], od_g_kvnorm[j], od_w_ukv[j],
                                od_g_sgu[j], od_w_spatial[j], od_b_spatial[j], od_w_out[j])
            new_ckv.append(ckv_p)
            new_kr.append(kr_p)
        xp = xp + mp[:, 2][:, None, :] * op
        xs = xs + ms[:, 2][:, None, :] * os_
        hp = adaln(xp, g_norm2[layer], mp[:, 3], mp[:, 4])
        hs = adaln(xs, g_norm2[layer], ms[:, 3], ms[:, 4])
        xp = xp + mp[:, 5][:, None, :] * peer(hp, peer_wq[layer], peer_subkeys[layer], peer_u[layer], peer_v[layer])
        xs = xs + ms[:, 5][:, None, :] * peer(hs, peer_wq[layer], peer_subkeys[layer], peer_u[layer], peer_v[layer])
    y_prompt = rmsnorm(xp, g_final)
    y_sample = rmsnorm(xs, g_final)
    new_attn_k = jnp.stack(new_k, axis=1)
    new_attn_v = jnp.stack(new_v, axis=1)
    new_mla_ckv = jnp.stack(new_ckv, axis=1)
    new_mla_krope = jnp.stack(new_kr, axis=1)
    return (y_prompt, y_sample, new_attn_k, new_attn_v, new_mla_ckv, new_mla_krope)
```

```python
import functools

import jax
import jax.numpy as jnp
from jax import lax
from jax.experimental import pallas as pl
from jax.experimental.pallas import tpu as pltpu

F32 = jnp.float32
BF16 = jnp.bfloat16

D_MODEL = 1024
BATCH = 32
SEQ = 256
DEC_BATCH = 2
DEC_SEQ = 2048
PAST_LEN = 512
GRID_W = 64
N_MOD = 6
RMS_EPS = 1e-6
ROPE_THETA = 10000.0
NEG_INF = -1e30

CONV_DIM = 512
HEAD_DIM_B = 64
N_HEADS_B = 8
N_KV_B = 2
WINDOW = 128
BAND_BLOCK = 128
EVEN_IN = 2304

N_HEADS_C = 8
Q_LORA = 384
KV_LORA = 256
NOPE_C = 64
ROPE_C = 32
V_DIM_C = 64
MLA_SCALE = (NOPE_C + ROPE_C) ** -0.5
CHUNK = 128
N_GROUPS_D = 4
D_D = 512
ODD_IN_PAD = 1792
KR_BLOCK = Q_LORA + KV_LORA + 2 * D_D
KR_LANE = NOPE_C
KR_COL = KR_BLOCK + KR_LANE
MLA_HEAD = 128

PEER_HEADS = 8
N_KEYS = 128
N_EXPERTS = N_KEYS * N_KEYS
PEER_TOPK = 16

T_PROMPT = BATCH * SEQ
T_SAMPLE = DEC_BATCH * DEC_SEQ
T_ALL = T_PROMPT + T_SAMPLE
N_GROUPS_MOD = 1 + DEC_BATCH

LANES = 128
ROW_BLOCK = 512
SEL_BLOCK = 128
PEER_TOKENS = 512
PEER_I1 = 8
VMEM_LIMIT = 56 * 1024 * 1024


def _params(sem, vmem=None):
    return pltpu.CompilerParams(dimension_semantics=sem, vmem_limit_bytes=vmem)


def _group_of_block(i, rows_per_block):
    n_prompt = T_PROMPT // rows_per_block
    per_batch = DEC_SEQ // rows_per_block
    return jnp.where(i < n_prompt, 0, 1 + (i - n_prompt) // per_batch)


def _mod_kernel(c_ref, w_ref, b_ref, o_ref):
    c = c_ref[...]
    s = c * jax.nn.sigmoid(c)
    o_ref[...] = jnp.dot(s.astype(BF16), w_ref[...].astype(BF16),
                         preferred_element_type=F32) + b_ref[...]


def _modulation(cvec8, w_mod, b_mod):
    n = N_MOD * D_MODEL
    return pl.pallas_call(
        _mod_kernel,
        grid=(N_MOD,),
        in_specs=[pl.BlockSpec((8, D_MODEL), lambda j: (0, 0)),
                  pl.BlockSpec((D_MODEL, D_MODEL), lambda j: (0, j)),
                  pl.BlockSpec((1, D_MODEL), lambda j: (0, j))],
        out_specs=pl.BlockSpec((8, D_MODEL), lambda j: (0, j)),
        out_shape=jax.ShapeDtypeStruct((8, n), F32),
        compiler_params=_params(("arbitrary",)),
        name="modulation",
    )(cvec8, w_mod, b_mod.reshape(1, n))


def _adaln(x, g, shift, scale):
    ms = jnp.mean(x * x, axis=-1, keepdims=True)
    y = x * lax.rsqrt(ms + RMS_EPS) * g
    return y * (1.0 + scale) + shift


def _ln_proj_kernel(x_ref, g_ref, shift_ref, scale_ref, w_ref, z_ref):
    h = _adaln(x_ref[...], g_ref[...], shift_ref[...], scale_ref[...])
    z_ref[...] = jnp.dot(h.astype(BF16), w_ref[...], preferred_element_type=F32)


def _ln_proj_t_kernel(x_ref, g_ref, shift_ref, scale_ref, w_ref, z_ref, ht_ref):
    h = _adaln(x_ref[...], g_ref[...], shift_ref[...], scale_ref[...])
    z = jnp.dot(h.astype(BF16), w_ref[...], preferred_element_type=F32)
    for k in range(z_ref.shape[0]):
        z_ref[k] = z[:, k * LANES:(k + 1) * LANES]
    ht_ref[...] = h.T.astype(BF16)


def _ln_proj(x, g, shift, scale, w, with_ht=False):
    n = w.shape[1]
    grp = lambda i: (_group_of_block(i, ROW_BLOCK), 0, 0)
    in_specs = [pl.BlockSpec((ROW_BLOCK, D_MODEL), lambda i: (i, 0)),
                pl.BlockSpec((1, D_MODEL), lambda i: (0, 0)),
                pl.BlockSpec((None, 1, D_MODEL), grp),
                pl.BlockSpec((None, 1, D_MODEL), grp),
                pl.BlockSpec((D_MODEL, n), lambda i: (0, 0))]
    z_spec = pl.BlockSpec((ROW_BLOCK, n), lambda i: (i, 0))
    z_shape = jax.ShapeDtypeStruct((T_ALL, n), F32)
    if with_ht:
        return pl.pallas_call(
            _ln_proj_t_kernel, grid=(T_ALL // ROW_BLOCK,), in_specs=in_specs,
            out_specs=[pl.BlockSpec((n // LANES, ROW_BLOCK, LANES), lambda i: (0, i, 0)),
                       pl.BlockSpec((D_MODEL, ROW_BLOCK), lambda i: (0, i))],
            out_shape=[jax.ShapeDtypeStruct((n // LANES, T_ALL, LANES), F32),
                       jax.ShapeDtypeStruct((D_MODEL, T_ALL), BF16)],
            compiler_params=_params(("arbitrary",), VMEM_LIMIT), name="ln_proj_t",
        )(x, g, shift, scale, w)
    return pl.pallas_call(
        _ln_proj_kernel, grid=(T_ALL // ROW_BLOCK,), in_specs=in_specs,
        out_specs=z_spec, out_shape=z_shape,
        compiler_params=_params(("arbitrary",), VMEM_LIMIT), name="ln_proj",
    )(x, g, shift, scale, w)


def _out_proj_kernel(ya_ref, yb_ref, wa_ref, wb_ref, x_ref, gate_ref, o_ref):
    o = jnp.dot(ya_ref[...], wa_ref[...], preferred_element_type=F32)
    o = o + jnp.dot(yb_ref[...], wb_ref[...], preferred_element_type=F32)
    o_ref[...] = x_ref[...] + gate_ref[...] * o


def _out_proj(ya, yb, wa, wb, x, gate):
    ka, kb = ya.shape[1], yb.shape[1]
    grp = lambda i: (_group_of_block(i, ROW_BLOCK), 0, 0)
    return pl.pallas_call(
        _out_proj_kernel, grid=(T_ALL // ROW_BLOCK,),
        in_specs=[pl.BlockSpec((ROW_BLOCK, ka), lambda i: (i, 0)),
                  pl.BlockSpec((ROW_BLOCK, kb), lambda i: (i, 0)),
                  pl.BlockSpec((ka, D_MODEL), lambda i: (0, 0)),
                  pl.BlockSpec((kb, D_MODEL), lambda i: (0, 0)),
                  pl.BlockSpec((ROW_BLOCK, D_MODEL), lambda i: (i, 0)),
                  pl.BlockSpec((None, 1, D_MODEL), grp)],
        out_specs=pl.BlockSpec((ROW_BLOCK, D_MODEL), lambda i: (i, 0)),
        out_shape=jax.ShapeDtypeStruct((T_ALL, D_MODEL), F32),
        compiler_params=_params(("arbitrary",), VMEM_LIMIT), name="out_proj",
    )(ya, yb, wa, wb, x, gate)


def _final_norm_kernel(x_ref, g_ref, o_ref):
    x = x_ref[...]
    ms = jnp.mean(x * x, axis=-1, keepdims=True)
    o_ref[...] = x * lax.rsqrt(ms + RMS_EPS) * g_ref[...]


def _final_norm(x, g):
    return pl.pallas_call(
        _final_norm_kernel, grid=(T_ALL // ROW_BLOCK,),
        in_specs=[pl.BlockSpec((ROW_BLOCK, D_MODEL), lambda i: (i, 0)),
                  pl.BlockSpec((1, D_MODEL), lambda i: (0, 0))],
        out_specs=pl.BlockSpec((ROW_BLOCK, D_MODEL), lambda i: (i, 0)),
        out_shape=jax.ShapeDtypeStruct((T_ALL, D_MODEL), F32),
        compiler_params=_params(("arbitrary",)), name="final_norm",
    )(x, g)


def _matmul_kernel(x_ref, w_ref, o_ref):
    o_ref[...] = jnp.dot(x_ref[...].astype(BF16), w_ref[...],
                         preferred_element_type=F32).astype(o_ref.dtype)


def _matmul(x, w, out_dtype, rows):
    m, k = x.shape
    n = w.shape[1]
    return pl.pallas_call(
        _matmul_kernel, grid=(m // rows,),
        in_specs=[pl.BlockSpec((rows, k), lambda i: (i, 0)),
                  pl.BlockSpec((k, n), lambda i: (0, 0))],
        out_specs=pl.BlockSpec((rows, n), lambda i: (i, 0)),
        out_shape=jax.ShapeDtypeStruct((m, n), out_dtype),
        compiler_params=_params(("arbitrary",)), name="matmul",
    )(x, w)


def _rope_tables(dr, n_heads):
    t = jnp.arange(DEC_SEQ)
    row = (t // GRID_W).astype(F32)
    col = (t % GRID_W).astype(F32)
    n_freq = dr // 4
    freqs = jnp.power(ROPE_THETA, -jnp.arange(n_freq, dtype=F32) / n_freq)
    ang = jnp.concatenate([row[:, None] * freqs, col[:, None] * freqs], axis=-1)
    cos, sin = jnp.cos(ang), jnp.sin(ang)
    cos_h = jnp.concatenate([cos, cos], axis=-1)
    sin_h = jnp.concatenate([-sin, sin], axis=-1)
    return jnp.tile(cos_h, (1, n_heads)), jnp.tile(sin_h, (1, n_heads))


def _mla_q_tables():
    cos, sin = _rope_tables(ROPE_C, 1)
    one = jnp.ones((DEC_SEQ, NOPE_C), F32)
    pad = MLA_HEAD - NOPE_C - ROPE_C
    cos_h = jnp.concatenate([one, cos, jnp.ones((DEC_SEQ, pad), F32)], axis=-1)
    sin_h = jnp.concatenate([0.0 * one, sin, jnp.zeros((DEC_SEQ, pad), F32)], axis=-1)
    return jnp.tile(cos_h, (1, N_HEADS_C)), jnp.tile(sin_h, (1, N_HEADS_C))


def _rope(x, cos, sin_signed, dr):
    n = x.shape[-1]
    half = dr // 2
    lane = lax.broadcasted_iota(jnp.int32, x.shape, x.ndim - 1)
    first = (lane % dr) < half
    partner = jnp.where(first, pltpu.roll(x, n - half, x.ndim - 1), pltpu.roll(x, half, x.ndim - 1))
    return x * cos + partner * sin_signed


def _conv_kernel(gb_ref, gc_ref, xa_ref, pgc_ref, pxa_ref, ngc_ref, nxa_ref, w_ref, o_ref, *, rows):
    i = pl.program_id(0)
    n_prompt = T_PROMPT // rows
    per_seq = DEC_SEQ // rows
    j = (i - n_prompt) % per_seq
    is_first = jnp.logical_or(i < n_prompt, j == 0)
    is_last = jnp.logical_or(i < n_prompt, j == per_seq - 1)
    u = gc_ref[...] * xa_ref[...]
    u_prev_row = jnp.where(is_first, 0.0, pgc_ref[7:8, :] * pxa_ref[7:8, :])
    u_next_row = jnp.where(is_last, 0.0, ngc_ref[0:1, :] * nxa_ref[0:1, :])
    rio = lax.broadcasted_iota(jnp.int32, u.shape, 0)
    up = jnp.where(rio == 0, u_prev_row, pltpu.roll(u, 1, 0))
    un = jnp.where(rio == rows - 1, u_next_row, pltpu.roll(u, rows - 1, 0))
    w = w_ref[...]
    y = gb_ref[...] * (up * w[0:1, :] + u * w[1:2, :] + un * w[2:3, :])
    o_ref[...] = y.astype(o_ref.dtype)


def _gated_conv(z, conv_w):
    rows = SEQ
    nb = T_ALL // rows
    r8 = rows // 8
    last8 = T_ALL // 8 - 1
    prev = lambda c: (lambda i: (jnp.maximum(i * r8 - 1, 0), c))
    nxt = lambda c: (lambda i: (jnp.minimum((i + 1) * r8, last8), c))
    return pl.pallas_call(
        functools.partial(_conv_kernel, rows=rows), grid=(nb,),
        in_specs=[pl.BlockSpec((rows, CONV_DIM), lambda i: (i, 0)),
                  pl.BlockSpec((rows, CONV_DIM), lambda i: (i, 1)),
                  pl.BlockSpec((rows, CONV_DIM), lambda i: (i, 2)),
                  pl.BlockSpec((8, CONV_DIM), prev(1)),
                  pl.BlockSpec((8, CONV_DIM), prev(2)),
                  pl.BlockSpec((8, CONV_DIM), nxt(1)),
                  pl.BlockSpec((8, CONV_DIM), nxt(2)),
                  pl.BlockSpec((3, CONV_DIM), lambda i: (0, 0))],
        out_specs=pl.BlockSpec((rows, CONV_DIM), lambda i: (i, 0)),
        out_shape=jax.ShapeDtypeStruct((T_ALL, CONV_DIM), BF16),
        compiler_params=_params(("arbitrary",)), name="gated_conv",
    )(z, z, z, z, z, z, z, conv_w)


def _nt_dot(a, b):
    return lax.dot_general(a, b, (((1,), (1,)), ((), ())), preferred_element_type=F32)


def _attn_ctx_b_kernel(sink_ref, q_ref, kv_ref, o_ref):
    scale = HEAD_DIM_B ** -0.5
    d = HEAD_DIM_B
    q = q_ref[...].astype(BF16)
    kv = kv_ref[...].astype(BF16)
    outs = []
    for h in range(N_HEADS_B):
        g = h // (N_HEADS_B // N_KV_B)
        k = kv[:, g * d:(g + 1) * d]
        v = kv[:, N_KV_B * d + g * d:N_KV_B * d + (g + 1) * d]
        s = _nt_dot(q[:, h * d:(h + 1) * d], k) * scale
        sk = sink_ref[h]
        m = jnp.maximum(jnp.max(s, axis=-1, keepdims=True), sk)
        p = jnp.exp(s - m)
        den = jnp.sum(p, axis=-1, keepdims=True) + jnp.exp(sk - m)
        o = jnp.dot(p.astype(BF16), v, preferred_element_type=F32) / den
        outs.append(o)
    o_ref[...] = jnp.concatenate(outs, axis=-1).astype(o_ref.dtype)


def _attn_ctx_b(z, sink):
    qc = 3 * CONV_DIM // (N_HEADS_B * HEAD_DIM_B)
    kc = (3 * CONV_DIM + N_HEADS_B * HEAD_DIM_B) // 256
    return pl.pallas_call(
        _attn_ctx_b_kernel, grid=(BATCH,),
        in_specs=[pl.BlockSpec(memory_space=pltpu.SMEM),
                  pl.BlockSpec((SEQ, 512), lambda b: (b, qc)),
                  pl.BlockSpec((SEQ, 256), lambda b: (b, kc))],
        out_specs=pl.BlockSpec((SEQ, 512), lambda b: (b, 0)),
        out_shape=jax.ShapeDtypeStruct((T_PROMPT, 512), BF16),
        compiler_params=_params(("arbitrary",)), name="attn_ctx_b",
    )(sink, z, z)


def _attn_lat_b_kernel(sink_ref, q_ref, kv_ref, kc_ref, vc_ref, cq_ref, sq_ref, ck_ref, sk_ref, o_ref):
    scale = HEAD_DIM_B ** -0.5
    d = HEAD_DIM_B
    n = pl.program_id(1)
    local = 3 * BAND_BLOCK
    start = jnp.clip((n - 1) * BAND_BLOCK, 0, DEC_SEQ - local)
    start = pl.multiple_of(start, BAND_BLOCK)
    q = _rope(q_ref[...], cq_ref[...], sq_ref[...], d).astype(BF16)
    kvw = kv_ref[pl.ds(start, local), :]
    kw = _rope(kvw[:, :N_KV_B * d], ck_ref[pl.ds(start, local), :], sk_ref[pl.ds(start, local), :], d)
    kw = kw.astype(BF16)
    vw = kvw[:, N_KV_B * d:].astype(BF16)
    kc = kc_ref[...].astype(BF16)
    vc = vc_ref[...].astype(BF16)
    qpos = n * BAND_BLOCK + lax.broadcasted_iota(jnp.int32, (BAND_BLOCK, local), 0)
    kpos = start + lax.broadcasted_iota(jnp.int32, (BAND_BLOCK, local), 1)
    in_window = jnp.abs(kpos - qpos) <= WINDOW
    outs = []
    for h in range(N_HEADS_B):
        g = h // (N_HEADS_B // N_KV_B)
        qh = q[:, h * d:(h + 1) * d]
        s_ctx = _nt_dot(qh, kc[:, g * d:(g + 1) * d]) * scale
        s_loc = _nt_dot(qh, kw[:, g * d:(g + 1) * d]) * scale
        s_loc = jnp.where(in_window, s_loc, NEG_INF)
        sk = sink_ref[h]
        m = jnp.maximum(jnp.maximum(jnp.max(s_ctx, axis=-1, keepdims=True),
                                    jnp.max(s_loc, axis=-1, keepdims=True)), sk)
        p_ctx = jnp.exp(s_ctx - m)
        p_loc = jnp.exp(s_loc - m)
        den = (jnp.sum(p_ctx, axis=-1, keepdims=True) + jnp.sum(p_loc, axis=-1, keepdims=True)
               + jnp.exp(sk - m))
        o = (jnp.dot(p_ctx.astype(BF16), vc[:, g * d:(g + 1) * d], preferred_element_type=F32)
             + jnp.dot(p_loc.astype(BF16), vw[:, g * d:(g + 1) * d], preferred_element_type=F32))
        outs.append(o / den)
    o_ref[...] = jnp.concatenate(outs, axis=-1).astype(o_ref.dtype)


def _attn_lat_b(z, sink, cache_k, cache_v, tables_q, tables_k):
    nb = DEC_SEQ // BAND_BLOCK
    qrow0 = T_PROMPT // BAND_BLOCK
    srow0 = T_PROMPT // DEC_SEQ
    qc = 3 * CONV_DIM // 512
    kc = (3 * CONV_DIM + N_HEADS_B * HEAD_DIM_B) // 256
    kvw = N_KV_B * HEAD_DIM_B
    return pl.pallas_call(
        _attn_lat_b_kernel, grid=(DEC_BATCH, nb),
        in_specs=[pl.BlockSpec(memory_space=pltpu.SMEM),
                  pl.BlockSpec((BAND_BLOCK, 512), lambda b, n: (qrow0 + b * nb + n, qc)),
                  pl.BlockSpec((DEC_SEQ, 256), lambda b, n: (srow0 + b, kc)),
                  pl.BlockSpec((None, PAST_LEN, kvw), lambda b, n: (b, 0, 0)),
                  pl.BlockSpec((None, PAST_LEN, kvw), lambda b, n: (b, 0, 0)),
                  pl.BlockSpec((BAND_BLOCK, 512), lambda b, n: (n, 0)),
                  pl.BlockSpec((BAND_BLOCK, 512), lambda b, n: (n, 0)),
                  pl.BlockSpec((DEC_SEQ, kvw), lambda b, n: (0, 0)),
                  pl.BlockSpec((DEC_SEQ, kvw), lambda b, n: (0, 0))],
        out_specs=pl.BlockSpec((BAND_BLOCK, 512), lambda b, n: (b * nb + n, 0)),
        out_shape=jax.ShapeDtypeStruct((T_SAMPLE, 512), BF16),
        compiler_params=_params(("arbitrary", "arbitrary")), name="attn_lat_b",
    )(sink, z, z, cache_k, cache_v, tables_q[0], tables_q[1], tables_k[0], tables_k[1])


def _rms(x, g):
    ms = jnp.mean(x * x, axis=-1, keepdims=True)
    return x * lax.rsqrt(ms + RMS_EPS) * g


def _odd_prep_kernel(z_ref, gq_ref, wuq_ref, gkv_ref, wukv_ref, gsgu_ref, ws_ref, bst_ref,
                     q_ref, ckv_ref, kv_ref, yd_ref, *, rows):
    z = z_ref[...]
    cq = z[:, :Q_LORA]
    ckv = z[:, Q_LORA:Q_LORA + KV_LORA]
    dz = z[:, Q_LORA + KV_LORA:KR_BLOCK]
    q_ref[...] = jnp.dot(_rms(cq, gq_ref[...]).astype(BF16), wuq_ref[...],
                         preferred_element_type=F32).astype(q_ref.dtype)
    ckv_n = _rms(ckv, gkv_ref[...])
    ckv_ref[...] = ckv_n
    kv_ref[...] = jnp.dot(ckv_n.astype(BF16), wukv_ref[...],
                          preferred_element_type=F32).astype(kv_ref.dtype)
    a = jax.nn.gelu(dz)
    u = a[:, :D_D]
    vv = _rms(a[:, D_D:], gsgu_ref[...]).astype(BF16)
    gd = D_D // N_GROUPS_D
    chunks = []
    for c in range(rows // CHUNK):
        groups = []
        for g in range(N_GROUPS_D):
            blk = vv[c * CHUNK:(c + 1) * CHUNK, g * gd:(g + 1) * gd]
            mixed = jnp.dot(ws_ref[g], blk, preferred_element_type=F32) + bst_ref[:, g:g + 1]
            groups.append(mixed)
        chunks.append(jnp.concatenate(groups, axis=-1))
    mixed = jnp.concatenate(chunks, axis=0)
    yd_ref[...] = (u * mixed).astype(yd_ref.dtype)


def _odd_prep(z, g_q, w_uq, g_kv, w_ukv, g_sgu, w_s, b_st):
    rows = SEQ
    full = lambda shape: pl.BlockSpec(shape, lambda i: tuple(0 for _ in shape))
    nq = w_uq.shape[1]
    nkv = w_ukv.shape[1]
    return pl.pallas_call(
        functools.partial(_odd_prep_kernel, rows=rows), grid=(T_ALL // rows,),
        in_specs=[pl.BlockSpec((rows, ODD_IN_PAD), lambda i: (i, 0)),
                  full((1, Q_LORA)), full((Q_LORA, nq)), full((1, KV_LORA)), full((KV_LORA, nkv)),
                  full((1, D_D)), full((N_GROUPS_D, CHUNK, CHUNK)), full((CHUNK, N_GROUPS_D))],
        out_specs=[pl.BlockSpec((rows, nq), lambda i: (i, 0)),
                   pl.BlockSpec((rows, KV_LORA), lambda i: (i, 0)),
                   pl.BlockSpec((rows, nkv), lambda i: (i, 0)),
                   pl.BlockSpec((rows, D_D), lambda i: (i, 0))],
        out_shape=[jax.ShapeDtypeStruct((T_ALL, nq), F32),
                   jax.ShapeDtypeStruct((T_ALL, KV_LORA), F32),
                   jax.ShapeDtypeStruct((T_ALL, nkv), BF16),
                   jax.ShapeDtypeStruct((T_ALL, D_D), BF16)],
        compiler_params=_params(("arbitrary",), VMEM_LIMIT), name="odd_prep",
    )(z, g_q, w_uq, g_kv, w_ukv, g_sgu, w_s, b_st)


def _attn_ctx_c_kernel(q_ref, kv_ref, kr_ref, o_ref):
    nk = N_HEADS_C * MLA_HEAD
    q = q_ref[...].astype(BF16)
    kv = kv_ref[...]
    kr = kr_ref[...].astype(BF16)
    outs = []
    for h in range(N_HEADS_C):
        k = kv[:, h * MLA_HEAD:(h + 1) * MLA_HEAD] + kr
        v = kv[:, nk + h * V_DIM_C:nk + (h + 1) * V_DIM_C]
        s = _nt_dot(q[:, h * MLA_HEAD:(h + 1) * MLA_HEAD], k) * MLA_SCALE
        m = jnp.max(s, axis=-1, keepdims=True)
        p = jnp.exp(s - m)
        den = jnp.sum(p, axis=-1, keepdims=True)
        outs.append(jnp.dot(p.astype(BF16), v, preferred_element_type=F32) / den)
    o_ref[...] = jnp.concatenate(outs, axis=-1).astype(o_ref.dtype)


def _attn_ctx_c(q, kv, z):
    return pl.pallas_call(
        _attn_ctx_c_kernel, grid=(BATCH,),
        in_specs=[pl.BlockSpec((SEQ, q.shape[1]), lambda b: (b, 0)),
                  pl.BlockSpec((SEQ, kv.shape[1]), lambda b: (b, 0)),
                  pl.BlockSpec((SEQ, LANES), lambda b: (b, KR_BLOCK // LANES))],
        out_specs=pl.BlockSpec((SEQ, 512), lambda b: (b, 0)),
        out_shape=jax.ShapeDtypeStruct((T_PROMPT, 512), BF16),
        compiler_params=_params(("arbitrary",)), name="attn_ctx_c",
    )(q, kv, z)


MLA_Q_BLOCK = 256


def _attn_lat_c_kernel(q_ref, kv_ref, kr_ref, kvc_ref, krc_ref, cq_ref, sq_ref, ck_ref, sk_ref, o_ref):
    nk = N_HEADS_C * MLA_HEAD
    q = _rope(q_ref[...], cq_ref[...], sq_ref[...], ROPE_C).astype(BF16)
    kr_l = _rope(kr_ref[...], ck_ref[...], sk_ref[...], ROPE_C).astype(BF16)
    kr_c = krc_ref[...].astype(BF16)
    kv_l = kv_ref[...]
    kv_c = kvc_ref[...]
    outs = []
    for h in range(N_HEADS_C):
        qh = q[:, h * MLA_HEAD:(h + 1) * MLA_HEAD]
        ks = slice(h * MLA_HEAD, (h + 1) * MLA_HEAD)
        vs = slice(nk + h * V_DIM_C, nk + (h + 1) * V_DIM_C)
        s_c = _nt_dot(qh, kv_c[:, ks] + kr_c) * MLA_SCALE
        s_l = _nt_dot(qh, kv_l[:, ks] + kr_l) * MLA_SCALE
        m = jnp.maximum(jnp.max(s_c, axis=-1, keepdims=True), jnp.max(s_l, axis=-1, keepdims=True))
        p_c = jnp.exp(s_c - m)
        p_l = jnp.exp(s_l - m)
        den = jnp.sum(p_c, axis=-1, keepdims=True) + jnp.sum(p_l, axis=-1, keepdims=True)
        o = (jnp.dot(p_c.astype(BF16), kv_c[:, vs], preferred_element_type=F32)
             + jnp.dot(p_l.astype(BF16), kv_l[:, vs], preferred_element_type=F32))
        outs.append(o / den)
    o_ref[...] = jnp.concatenate(outs, axis=-1).astype(o_ref.dtype)


def _attn_lat_c(q, kv, z, kv_cache, kr_cache, tables_q, tables_k):
    tq = MLA_Q_BLOCK
    nb = DEC_SEQ // tq
    qrow0 = T_PROMPT // tq
    srow0 = T_PROMPT // DEC_SEQ
    nr = q.shape[1]
    return pl.pallas_call(
        _attn_lat_c_kernel, grid=(DEC_BATCH, nb),
        in_specs=[pl.BlockSpec((tq, q.shape[1]), lambda b, n: (qrow0 + b * nb + n, 0)),
                  pl.BlockSpec((DEC_SEQ, kv.shape[1]), lambda b, n: (srow0 + b, 0)),
                  pl.BlockSpec((DEC_SEQ, LANES), lambda b, n: (srow0 + b, KR_BLOCK // LANES)),
                  pl.BlockSpec((None, PAST_LEN, kv.shape[1]), lambda b, n: (b, 0, 0)),
                  pl.BlockSpec((None, PAST_LEN, LANES), lambda b, n: (b, 0, 0)),
                  pl.BlockSpec((tq, nr), lambda b, n: (n, 0)),
                  pl.BlockSpec((tq, nr), lambda b, n: (n, 0)),
                  pl.BlockSpec((DEC_SEQ, LANES), lambda b, n: (0, 0)),
                  pl.BlockSpec((DEC_SEQ, LANES), lambda b, n: (0, 0))],
        out_specs=pl.BlockSpec((tq, 512), lambda b, n: (b * nb + n, 0)),
        out_shape=jax.ShapeDtypeStruct((T_SAMPLE, 512), BF16),
        compiler_params=_params(("arbitrary", "arbitrary"), VMEM_LIMIT), name="attn_lat_c",
    )(q, kv, z, kv_cache, kr_cache, tables_q[0], tables_q[1], tables_k[0], tables_k[1])


def _top16(s):
    kio = lax.broadcasted_iota(jnp.int32, s.shape, 0).astype(F32)
    rio = lax.broadcasted_iota(jnp.int32, (PEER_TOPK, s.shape[1]), 0)
    rank = jnp.full(s.shape, float(PEER_TOPK), F32)
    vals = jnp.zeros((PEER_TOPK, s.shape[1]), F32)
    x = s
    for it in range(PEER_TOPK):
        m = jnp.max(x, axis=0, keepdims=True)
        first = jnp.min(jnp.where(x == m, kio, float(N_KEYS)), axis=0, keepdims=True)
        hit = kio == first
        rank = jnp.where(hit, float(it), rank)
        x = jnp.where(hit, -jnp.inf, x)
        vals = jnp.where(rio == it, m, vals)
    return rank, vals


def _pair_counts(sv1, sv2):
    aio = lax.broadcasted_iota(jnp.int32, sv1.shape, 0).astype(F32)
    cnt = jnp.zeros(sv1.shape, F32)
    front = sv1 + sv2[0:1, :]
    top = front[0:1, :]
    z = jnp.zeros_like(top)
    for _ in range(PEER_TOPK):
        m = jnp.max(front, axis=0, keepdims=True)
        first = jnp.min(jnp.where(front == m, aio, float(PEER_TOPK)), axis=0, keepdims=True)
        hit = aio == first
        z = z + jnp.exp(m - top)
        cnt = jnp.where(hit, cnt + 1.0, cnt)
        nxt = jnp.full(sv1.shape, -jnp.inf, F32)
        for b in range(1, PEER_TOPK):
            nxt = jnp.where(cnt == float(b), sv2[b:b + 1, :], nxt)
        front = jnp.where(hit, sv1 + nxt, front)
    return cnt, z


def _peer_select_kernel(q_ref, keys_ref, r2_ref, e2_ref, cnt_ref, e1g_ref):
    def head(h, carry):
        def side(s):
            sc = _nt_dot(keys_ref[2 * h + s], q_ref[2 * h + s].astype(BF16))
            rank, vals = _top16(sc)
            return sc, rank, vals
        s1, rank1, sv1 = side(0)
        s2, rank2, sv2 = side(1)
        cnt, z = _pair_counts(sv1, sv2)
        cnt_keys = jnp.zeros_like(s1)
        for a in range(PEER_TOPK):
            cnt_keys = jnp.where(rank1 == float(a), cnt[a:a + 1, :], cnt_keys)
        r2_ref[h] = rank2
        e2_ref[h] = jnp.exp(s2 - sv2[0:1, :])
        cnt_ref[h] = cnt_keys
        e1g_ref[h] = jnp.exp(s1 - sv1[0:1, :]) / z
        return carry
    lax.fori_loop(0, PEER_HEADS, head, 0)


def _peer_select(qp, subkeys):
    shp = jax.ShapeDtypeStruct((PEER_HEADS, N_KEYS, T_ALL), F32)
    ospec = pl.BlockSpec((PEER_HEADS, N_KEYS, SEL_BLOCK), lambda i: (0, 0, i))
    return pl.pallas_call(
        _peer_select_kernel, grid=(T_ALL // SEL_BLOCK,),
        in_specs=[pl.BlockSpec((2 * PEER_HEADS, SEL_BLOCK, N_KEYS), lambda i: (0, i, 0)),
                  pl.BlockSpec(subkeys.shape, lambda i: (0, 0, 0))],
        out_specs=[ospec, ospec, ospec, ospec],
        out_shape=[shp, shp, shp, shp],
        compiler_params=_params(("arbitrary",)), name="peer_select",
    )(qp, subkeys)


def _peer_dense_kernel(ht_ref, u_ref, vt_ref, r2_ref, e2_ref, cnt_ref, e1g_ref, x_ref, gate_ref,
                       o_ref, acc_ref, p_ref):
    j = pl.program_id(1)

    @pl.when(j == 0)
    def _():
        acc_ref[...] = jnp.zeros_like(acc_ref)

    ht = ht_ref[...]
    for a in range(PEER_I1):
        act = jnp.dot(u_ref[a * N_KEYS:(a + 1) * N_KEYS, :], ht, preferred_element_type=F32)
        w = jnp.zeros_like(act)
        for h in range(PEER_HEADS):
            taken = r2_ref[h] < cnt_ref[h, a:a + 1, :]
            w = w + jnp.where(taken, e2_ref[h] * e1g_ref[h, a:a + 1, :], 0.0)
        p_ref[a * N_KEYS:(a + 1) * N_KEYS, :] = (w * jax.nn.gelu(act)).astype(BF16)
    acc_ref[...] += jnp.dot(vt_ref[...], p_ref[...], preferred_element_type=F32)

    @pl.when(j == pl.num_programs(1) - 1)
    def _():
        o_ref[...] = x_ref[...] + gate_ref[...] * acc_ref[...].T


def _peer_dense(ht, u_bf, vt_bf, r2, e2, cnt, e1g, x, gate):
    tb = PEER_TOKENS
    et = PEER_I1 * N_KEYS
    grp = lambda i, j: (_group_of_block(i, tb), 0, 0)
    tok = pl.BlockSpec((PEER_HEADS, N_KEYS, tb), lambda i, j: (0, 0, i))
    key1 = pl.BlockSpec((PEER_HEADS, PEER_I1, tb), lambda i, j: (0, j, i))
    return pl.pallas_call(
        _peer_dense_kernel, grid=(T_ALL // tb, N_EXPERTS // et),
        in_specs=[pl.BlockSpec((D_MODEL, tb), lambda i, j: (0, i)),
                  pl.BlockSpec((et, D_MODEL), lambda i, j: (j, 0)),
                  pl.BlockSpec((D_MODEL, et), lambda i, j: (0, j)),
                  tok, tok, key1, key1,
                  pl.BlockSpec((tb, D_MODEL), lambda i, j: (i, 0)),
                  pl.BlockSpec((None, 1, D_MODEL), grp)],
        out_specs=pl.BlockSpec((tb, D_MODEL), lambda i, j: (i, 0)),
        out_shape=jax.ShapeDtypeStruct((T_ALL, D_MODEL), F32),
        scratch_shapes=[pltpu.VMEM((D_MODEL, tb), F32), pltpu.VMEM((et, tb), BF16)],
        compiler_params=_params(("arbitrary", "arbitrary"), VMEM_LIMIT), name="peer_dense",
    )(ht, u_bf, vt_bf, r2, e2, cnt, e1g, x, gate)


def _peer_layer(x, g2, shift, scale, gate, wq_bf, subkeys_bf, u_bf, vt_bf):
    qp, ht = _ln_proj(x, g2, shift, scale, wq_bf, with_ht=True)
    r2, e2, cnt, e1g = _peer_select(qp, subkeys_bf)
    return _peer_dense(ht, u_bf, vt_bf, r2, e2, cnt, e1g, x, gate)


def kernel(x_prompt, x_sample, cache_attn_k, cache_attn_v, cache_mla_ckv, cache_mla_krope, c, c_ctx, w_mod, b_mod, g_norm1, g_norm2, ev_w_in, ev_conv_w, ev_sink, ev_w_out, od_w_in, od_g_qnorm, od_w_uq, od_g_kvnorm, od_w_ukv, od_g_sgu, od_w_spatial, od_b_spatial, od_w_out, peer_wq, peer_subkeys, peer_u, peer_v, g_final):
    x = jnp.concatenate([x_prompt.reshape(T_PROMPT, D_MODEL), x_sample.reshape(T_SAMPLE, D_MODEL)], axis=0)
    cvec = jnp.concatenate([c_ctx[None, :], c, jnp.zeros((8 - N_GROUPS_MOD, D_MODEL), F32)], axis=0)

    def mods(layer):
        m = _modulation(cvec, w_mod[layer], b_mod[layer])[:N_GROUPS_MOD]
        m = m.reshape(N_GROUPS_MOD, N_MOD, 1, D_MODEL)
        return [m[:, k] for k in range(N_MOD)]

    def peer(layer, x, m):
        keys = peer_subkeys[layer].reshape(PEER_HEADS * 2, N_KEYS, N_KEYS).astype(BF16)
        return _peer_layer(x, g_norm2[layer][None, :], m[3], m[4], m[5],
                           peer_wq[layer].astype(BF16), keys,
                           peer_u[layer].astype(BF16), peer_v[layer].T.astype(BF16))

    m = mods(0)
    z0 = _ln_proj(x, g_norm1[0][None, :], m[0], m[1], ev_w_in[0].astype(BF16))
    y_a = _gated_conv(z0, ev_conv_w[0])
    kvw = N_KV_B * HEAD_DIM_B
    yb_p = _attn_ctx_b(z0, ev_sink[0])
    yb_s = _attn_lat_b(z0, ev_sink[0],
                       cache_attn_k[:, 0].reshape(DEC_BATCH, PAST_LEN, kvw),
                       cache_attn_v[:, 0].reshape(DEC_BATCH, PAST_LEN, kvw),
                       _rope_tables(HEAD_DIM_B, N_HEADS_B), _rope_tables(HEAD_DIM_B, N_KV_B))
    y_b = jnp.concatenate([yb_p, yb_s], axis=0)
    w_out = ev_w_out[0].astype(BF16)
    x = _out_proj(y_a, y_b, w_out[:CONV_DIM], w_out[CONV_DIM:], x, m[2])
    x = peer(0, x, m)
    kcol = 3 * CONV_DIM + N_HEADS_B * HEAD_DIM_B
    new_attn_k = z0[:T_PROMPT, kcol:kcol + kvw].reshape(BATCH, 1, SEQ, N_KV_B, HEAD_DIM_B)
    new_attn_v = z0[:T_PROMPT, kcol + kvw:kcol + 2 * kvw].reshape(BATCH, 1, SEQ, N_KV_B, HEAD_DIM_B)

    m = mods(1)
    w_in = od_w_in[0]
    w_in = jnp.concatenate([w_in[:, :Q_LORA + KV_LORA], w_in[:, Q_LORA + KV_LORA + ROPE_C:],
                            jnp.zeros((D_MODEL, KR_LANE), F32),
                            w_in[:, Q_LORA + KV_LORA:Q_LORA + KV_LORA + ROPE_C],
                            jnp.zeros((D_MODEL, ODD_IN_PAD - KR_COL - ROPE_C), F32)], axis=1)
    z1 = _ln_proj(x, g_norm1[1][None, :], m[0], m[1], w_in.astype(BF16))
    w_uq = od_w_uq[0].reshape(Q_LORA, N_HEADS_C, NOPE_C + ROPE_C)
    w_uq = jnp.pad(w_uq, ((0, 0), (0, 0), (0, MLA_HEAD - NOPE_C - ROPE_C)))
    w_uq = w_uq.reshape(Q_LORA, N_HEADS_C * MLA_HEAD).astype(BF16)
    w_ukv = od_w_ukv[0].reshape(KV_LORA, N_HEADS_C, NOPE_C + V_DIM_C)
    w_uk = jnp.pad(w_ukv[:, :, :NOPE_C], ((0, 0), (0, 0), (0, MLA_HEAD - NOPE_C)))
    w_ukv = jnp.concatenate([w_uk.reshape(KV_LORA, -1), w_ukv[:, :, NOPE_C:].reshape(KV_LORA, -1)],
                            axis=1).astype(BF16)
    q, ckv_n, kv, y_d = _odd_prep(z1, od_g_qnorm[0][None, :], w_uq, od_g_kvnorm[0][None, :], w_ukv,
                                  od_g_sgu[0][None, :], od_w_spatial[0].astype(BF16), od_b_spatial[0].T)
    kv_cache = _matmul(cache_mla_ckv[:, 0].reshape(DEC_BATCH * PAST_LEN, KV_LORA), w_ukv, BF16, PAST_LEN)
    kr_cache = jnp.pad(cache_mla_krope[:, 0], ((0, 0), (0, 0), (KR_LANE, LANES - KR_LANE - ROPE_C)))
    yc_p = _attn_ctx_c(q, kv, z1)
    yc_s = _attn_lat_c(q, kv, z1, kv_cache.reshape(DEC_BATCH, PAST_LEN, -1), kr_cache,
                       _mla_q_tables(), _rope_tables(ROPE_C, LANES // ROPE_C))
    y_c = jnp.concatenate([yc_p, yc_s], axis=0)
    w_out = od_w_out[0].astype(BF16)
    x = _out_proj(y_c, y_d, w_out[:N_HEADS_C * V_DIM_C], w_out[N_HEADS_C * V_DIM_C:], x, m[2])
    x = peer(1, x, m)
    new_mla_ckv = ckv_n[:T_PROMPT].reshape(BATCH, 1, SEQ, KV_LORA)
    new_mla_krope = z1[:T_PROMPT, KR_COL:KR_COL + ROPE_C].reshape(BATCH, 1, SEQ, ROPE_C)

    y = _final_norm(x, g_final[None, :])
    y_prompt = y[:T_PROMPT].reshape(BATCH, SEQ, D_MODEL)
    y_sample = y[T_PROMPT:].reshape(DEC_BATCH, DEC_SEQ, D_MODEL)
    return (y_prompt, y_sample, new_attn_k, new_attn_v, new_mla_ckv, new_mla_krope)
```

```python
import functools

import jax
import jax.numpy as jnp
from jax import lax
from jax.experimental import pallas as pl
from jax.experimental.pallas import tpu as pltpu

F32 = jnp.float32
BF16 = jnp.bfloat16

D_MODEL = 1024
BATCH = 32
SEQ = 256
DEC_BATCH = 2
DEC_SEQ = 2048
PAST_LEN = 512
GRID_W = 64
N_MOD = 6
RMS_EPS = 1e-6
ROPE_THETA = 10000.0
NEG_INF = -1e30

CONV_DIM = 512
HEAD_DIM_B = 64
N_HEADS_B = 8
N_KV_B = 2
WINDOW = 128
BAND_BLOCK = 128
EVEN_IN = 2304

N_HEADS_C = 8
Q_LORA = 384
KV_LORA = 256
NOPE_C = 64
ROPE_C = 32
V_DIM_C = 64
MLA_SCALE = (NOPE_C + ROPE_C) ** -0.5
CHUNK = 128
N_GROUPS_D = 4
D_D = 512
ODD_IN_PAD = 1792
KR_BLOCK = Q_LORA + KV_LORA + 2 * D_D
KR_LANE = NOPE_C
KR_COL = KR_BLOCK + KR_LANE
MLA_HEAD = 128

PEER_HEADS = 8
N_KEYS = 128
N_EXPERTS = N_KEYS * N_KEYS
PEER_TOPK = 16

T_PROMPT = BATCH * SEQ
T_SAMPLE = DEC_BATCH * DEC_SEQ
T_ALL = T_PROMPT + T_SAMPLE
N_GROUPS_MOD = 1 + DEC_BATCH

LANES = 128
ROW_BLOCK = 512
SEL_BLOCK = 128
PEER_TOKENS = 512
PEER_I1 = 8
VMEM_LIMIT = 56 * 1024 * 1024


def _params(sem, vmem=None):
    return pltpu.CompilerParams(dimension_semantics=sem, vmem_limit_bytes=vmem)


def _group_of_block(i, rows_per_block):
    n_prompt = T_PROMPT // rows_per_block
    per_batch = DEC_SEQ // rows_per_block
    return jnp.where(i < n_prompt, 0, 1 + (i - n_prompt) // per_batch)


def _mod_kernel(c_ref, w_ref, b_ref, o_ref):
    c = c_ref[...]
    s = c * jax.nn.sigmoid(c)
    o_ref[...] = jnp.dot(s.astype(BF16), w_ref[...].astype(BF16),
                         preferred_element_type=F32) + b_ref[...]


def _modulation(cvec8, w_mod, b_mod):
    n = N_MOD * D_MODEL
    return pl.pallas_call(
        _mod_kernel,
        grid=(N_MOD,),
        in_specs=[pl.BlockSpec((8, D_MODEL), lambda j: (0, 0)),
                  pl.BlockSpec((D_MODEL, D_MODEL), lambda j: (0, j)),
                  pl.BlockSpec((1, D_MODEL), lambda j: (0, j))],
        out_specs=pl.BlockSpec((8, D_MODEL), lambda j: (0, j)),
        out_shape=jax.ShapeDtypeStruct((8, n), F32),
        compiler_params=_params(("arbitrary",)),
        name="modulation",
    )(cvec8, w_mod, b_mod.reshape(1, n))


def _adaln(x, g, shift, scale):
    ms = jnp.mean(x * x, axis=-1, keepdims=True)
    y = x * lax.rsqrt(ms + RMS_EPS) * g
    return y * (1.0 + scale) + shift


def _ln_proj_kernel(x_ref, g_ref, shift_ref, scale_ref, w_ref, z_ref):
    h = _adaln(x_ref[...], g_ref[...], shift_ref[...], scale_ref[...])
    z_ref[...] = jnp.dot(h.astype(BF16), w_ref[...], preferred_element_type=F32)


def _ln_proj_t_kernel(x_ref, g_ref, shift_ref, scale_ref, w_ref, z_ref, ht_ref):
    h = _adaln(x_ref[...], g_ref[...], shift_ref[...], scale_ref[...])
    z = jnp.dot(h.astype(BF16), w_ref[...], preferred_element_type=F32)
    for k in range(z_ref.shape[0]):
        z_ref[k] = z[:, k * LANES:(k + 1) * LANES]
    ht_ref[...] = h.T.astype(BF16)


def _ln_proj(x, g, shift, scale, w, with_ht=False):
    n = w.shape[1]
    grp = lambda i: (_group_of_block(i, ROW_BLOCK), 0, 0)
    in_specs = [pl.BlockSpec((ROW_BLOCK, D_MODEL), lambda i: (i, 0)),
                pl.BlockSpec((1, D_MODEL), lambda i: (0, 0)),
                pl.BlockSpec((None, 1, D_MODEL), grp),
                pl.BlockSpec((None, 1, D_MODEL), grp),
                pl.BlockSpec((D_MODEL, n), lambda i: (0, 0))]
    z_spec = pl.BlockSpec((ROW_BLOCK, n), lambda i: (i, 0))
    z_shape = jax.ShapeDtypeStruct((T_ALL, n), F32)
    if with_ht:
        return pl.pallas_call(
            _ln_proj_t_kernel, grid=(T_ALL // ROW_BLOCK,), in_specs=in_specs,
            out_specs=[pl.BlockSpec((n // LANES, ROW_BLOCK, LANES), lambda i: (0, i, 0)),
                       pl.BlockSpec((D_MODEL, ROW_BLOCK), lambda i: (0, i))],
            out_shape=[jax.ShapeDtypeStruct((n // LANES, T_ALL, LANES), F32),
                       jax.ShapeDtypeStruct((D_MODEL, T_ALL), BF16)],
            compiler_params=_params(("arbitrary",), VMEM_LIMIT), name="ln_proj_t",
        )(x, g, shift, scale, w)
    return pl.pallas_call(
        _ln_proj_kernel, grid=(T_ALL // ROW_BLOCK,), in_specs=in_specs,
        out_specs=z_spec, out_shape=z_shape,
        compiler_params=_params(("arbitrary",), VMEM_LIMIT), name="ln_proj",
    )(x, g, shift, scale, w)


def _out_proj_kernel(ya_ref, yb_ref, wa_ref, wb_ref, x_ref, gate_ref, o_ref):
    o = jnp.dot(ya_ref[...], wa_ref[...], preferred_element_type=F32)
    o = o + jnp.dot(yb_ref[...], wb_ref[...], preferred_element_type=F32)
    o_ref[...] = x_ref[...] + gate_ref[...] * o


def _out_proj(ya, yb, wa, wb, x, gate):
    ka, kb = ya.shape[1], yb.shape[1]
    grp = lambda i: (_group_of_block(i, ROW_BLOCK), 0, 0)
    return pl.pallas_call(
        _out_proj_kernel, grid=(T_ALL // ROW_BLOCK,),
        in_specs=[pl.BlockSpec((ROW_BLOCK, ka), lambda i: (i, 0)),
                  pl.BlockSpec((ROW_BLOCK, kb), lambda i: (i, 0)),
                  pl.BlockSpec((ka, D_MODEL), lambda i: (0, 0)),
                  pl.BlockSpec((kb, D_MODEL), lambda i: (0, 0)),
                  pl.BlockSpec((ROW_BLOCK, D_MODEL), lambda i: (i, 0)),
                  pl.BlockSpec((None, 1, D_MODEL), grp)],
        out_specs=pl.BlockSpec((ROW_BLOCK, D_MODEL), lambda i: (i, 0)),
        out_shape=jax.ShapeDtypeStruct((T_ALL, D_MODEL), F32),
        compiler_params=_params(("arbitrary",), VMEM_LIMIT), name="out_proj",
    )(ya, yb, wa, wb, x, gate)


def _final_norm_kernel(x_ref, g_ref, o_ref):
    x = x_ref[...]
    ms = jnp.mean(x * x, axis=-1, keepdims=True)
    o_ref[...] = x * lax.rsqrt(ms + RMS_EPS) * g_ref[...]


def _final_norm(x, g):
    return pl.pallas_call(
        _final_norm_kernel, grid=(T_ALL // ROW_BLOCK,),
        in_specs=[pl.BlockSpec((ROW_BLOCK, D_MODEL), lambda i: (i, 0)),
                  pl.BlockSpec((1, D_MODEL), lambda i: (0, 0))],
        out_specs=pl.BlockSpec((ROW_BLOCK, D_MODEL), lambda i: (i, 0)),
        out_shape=jax.ShapeDtypeStruct((T_ALL, D_MODEL), F32),
        compiler_params=_params(("arbitrary",)), name="final_norm",
    )(x, g)


def _matmul_kernel(x_ref, w_ref, o_ref):
    o_ref[...] = jnp.dot(x_ref[...].astype(BF16), w_ref[...],
                         preferred_element_type=F32).astype(o_ref.dtype)


def _matmul(x, w, out_dtype, rows):
    m, k = x.shape
    n = w.shape[1]
    return pl.pallas_call(
        _matmul_kernel, grid=(m // rows,),
        in_specs=[pl.BlockSpec((rows, k), lambda i: (i, 0)),
                  pl.BlockSpec((k, n), lambda i: (0, 0))],
        out_specs=pl.BlockSpec((rows, n), lambda i: (i, 0)),
        out_shape=jax.ShapeDtypeStruct((m, n), out_dtype),
        compiler_params=_params(("arbitrary",)), name="matmul",
    )(x, w)


def _rope_tables(dr, n_heads):
    t = jnp.arange(DEC_SEQ)
    row = (t // GRID_W).astype(F32)
    col = (t % GRID_W).astype(F32)
    n_freq = dr // 4
    freqs = jnp.power(ROPE_THETA, -jnp.arange(n_freq, dtype=F32) / n_freq)
    ang = jnp.concatenate([row[:, None] * freqs, col[:, None] * freqs], axis=-1)
    cos, sin = jnp.cos(ang), jnp.sin(ang)
    cos_h = jnp.concatenate([cos, cos], axis=-1)
    sin_h = jnp.concatenate([-sin, sin], axis=-1)
    return jnp.tile(cos_h, (1, n_heads)), jnp.tile(sin_h, (1, n_heads))


def _mla_q_tables():
    cos, sin = _rope_tables(ROPE_C, 1)
    one = jnp.ones((DEC_SEQ, NOPE_C), F32)
    pad = MLA_HEAD - NOPE_C - ROPE_C
    cos_h = jnp.concatenate([one, cos, jnp.ones((DEC_SEQ, pad), F32)], axis=-1)
    sin_h = jnp.concatenate([0.0 * one, sin, jnp.zeros((DEC_SEQ, pad), F32)], axis=-1)
    return jnp.tile(cos_h, (1, N_HEADS_C)), jnp.tile(sin_h, (1, N_HEADS_C))


def _rope(x, cos, sin_signed, dr):
    n = x.shape[-1]
    half = dr // 2
    lane = lax.broadcasted_iota(jnp.int32, x.shape, x.ndim - 1)
    first = (lane % dr) < half
    partner = jnp.where(first, pltpu.roll(x, n - half, x.ndim - 1), pltpu.roll(x, half, x.ndim - 1))
    return x * cos + partner * sin_signed


def _conv_kernel(gb_ref, gc_ref, xa_ref, pgc_ref, pxa_ref, ngc_ref, nxa_ref, w_ref, o_ref, *, rows):
    i = pl.program_id(0)
    n_prompt = T_PROMPT // rows
    per_seq = DEC_SEQ // rows
    j = (i - n_prompt) % per_seq
    is_first = jnp.logical_or(i < n_prompt, j == 0)
    is_last = jnp.logical_or(i < n_prompt, j == per_seq - 1)
    u = gc_ref[...] * xa_ref[...]
    u_prev_row = jnp.where(is_first, 0.0, pgc_ref[7:8, :] * pxa_ref[7:8, :])
    u_next_row = jnp.where(is_last, 0.0, ngc_ref[0:1, :] * nxa_ref[0:1, :])
    rio = lax.broadcasted_iota(jnp.int32, u.shape, 0)
    up = jnp.where(rio == 0, u_prev_row, pltpu.roll(u, 1, 0))
    un = jnp.where(rio == rows - 1, u_next_row, pltpu.roll(u, rows - 1, 0))
    w = w_ref[...]
    y = gb_ref[...] * (up * w[0:1, :] + u * w[1:2, :] + un * w[2:3, :])
    o_ref[...] = y.astype(o_ref.dtype)


def _gated_conv(z, conv_w):
    rows = SEQ
    nb = T_ALL // rows
    r8 = rows // 8
    last8 = T_ALL // 8 - 1
    prev = lambda c: (lambda i: (jnp.maximum(i * r8 - 1, 0), c))
    nxt = lambda c: (lambda i: (jnp.minimum((i + 1) * r8, last8), c))
    return pl.pallas_call(
        functools.partial(_conv_kernel, rows=rows), grid=(nb,),
        in_specs=[pl.BlockSpec((rows, CONV_DIM), lambda i: (i, 0)),
                  pl.BlockSpec((rows, CONV_DIM), lambda i: (i, 1)),
                  pl.BlockSpec((rows, CONV_DIM), lambda i: (i, 2)),
                  pl.BlockSpec((8, CONV_DIM), prev(1)),
                  pl.BlockSpec((8, CONV_DIM), prev(2)),
                  pl.BlockSpec((8, CONV_DIM), nxt(1)),
                  pl.BlockSpec((8, CONV_DIM), nxt(2)),
                  pl.BlockSpec((3, CONV_DIM), lambda i: (0, 0))],
        out_specs=pl.BlockSpec((rows, CONV_DIM), lambda i: (i, 0)),
        out_shape=jax.ShapeDtypeStruct((T_ALL, CONV_DIM), BF16),
        compiler_params=_params(("arbitrary",)), name="gated_conv",
    )(z, z, z, z, z, z, z, conv_w)


def _nt_dot(a, b):
    return lax.dot_general(a, b, (((1,), (1,)), ((), ())), preferred_element_type=F32)


def _attn_ctx_b_kernel(sink_ref, q_ref, kv_ref, o_ref):
    scale = HEAD_DIM_B ** -0.5
    d = HEAD_DIM_B
    q = q_ref[...].astype(BF16)
    kv = kv_ref[...].astype(BF16)
    outs = []
    for h in range(N_HEADS_B):
        g = h // (N_HEADS_B // N_KV_B)
        k = kv[:, g * d:(g + 1) * d]
        v = kv[:, N_KV_B * d + g * d:N_KV_B * d + (g + 1) * d]
        s = _nt_dot(q[:, h * d:(h + 1) * d], k) * scale
        sk = sink_ref[h]
        m = jnp.maximum(jnp.max(s, axis=-1, keepdims=True), sk)
        p = jnp.exp(s - m)
        den = jnp.sum(p, axis=-1, keepdims=True) + jnp.exp(sk - m)
        o = jnp.dot(p.astype(BF16), v, preferred_element_type=F32) / den
        outs.append(o)
    o_ref[...] = jnp.concatenate(outs, axis=-1).astype(o_ref.dtype)


def _attn_ctx_b(z, sink):
    qc = 3 * CONV_DIM // (N_HEADS_B * HEAD_DIM_B)
    kc = (3 * CONV_DIM + N_HEADS_B * HEAD_DIM_B) // 256
    return pl.pallas_call(
        _attn_ctx_b_kernel, grid=(BATCH,),
        in_specs=[pl.BlockSpec(memory_space=pltpu.SMEM),
                  pl.BlockSpec((SEQ, 512), lambda b: (b, qc)),
                  pl.BlockSpec((SEQ, 256), lambda b: (b, kc))],
        out_specs=pl.BlockSpec((SEQ, 512), lambda b: (b, 0)),
        out_shape=jax.ShapeDtypeStruct((T_PROMPT, 512), BF16),
        compiler_params=_params(("arbitrary",)), name="attn_ctx_b",
    )(sink, z, z)


def _attn_lat_b_kernel(sink_ref, q_ref, kv_ref, kc_ref, vc_ref, cq_ref, sq_ref, ck_ref, sk_ref, o_ref):
    scale = HEAD_DIM_B ** -0.5
    d = HEAD_DIM_B
    n = pl.program_id(1)
    local = 3 * BAND_BLOCK
    start = jnp.clip((n - 1) * BAND_BLOCK, 0, DEC_SEQ - local)
    start = pl.multiple_of(start, BAND_BLOCK)
    q = _rope(q_ref[...], cq_ref[...], sq_ref[...], d).astype(BF16)
    kvw = kv_ref[pl.ds(start, local), :]
    kw = _rope(kvw[:, :N_KV_B * d], ck_ref[pl.ds(start, local), :], sk_ref[pl.ds(start, local), :], d)
    kw = kw.astype(BF16)
    vw = kvw[:, N_KV_B * d:].astype(BF16)
    kc = kc_ref[...].astype(BF16)
    vc = vc_ref[...].astype(BF16)
    qpos = n * BAND_BLOCK + lax.broadcasted_iota(jnp.int32, (BAND_BLOCK, local), 0)
    kpos = start + lax.broadcasted_iota(jnp.int32, (BAND_BLOCK, local), 1)
    in_window = jnp.abs(kpos - qpos) <= WINDOW
    outs = []
    for h in range(N_HEADS_B):
        g = h // (N_HEADS_B // N_KV_B)
        qh = q[:, h * d:(h + 1) * d]
        s_ctx = _nt_dot(qh, kc[:, g * d:(g + 1) * d]) * scale
        s_loc = _nt_dot(qh, kw[:, g * d:(g + 1) * d]) * scale
        s_loc = jnp.where(in_window, s_loc, NEG_INF)
        sk = sink_ref[h]
        m = jnp.maximum(jnp.maximum(jnp.max(s_ctx, axis=-1, keepdims=True),
                                    jnp.max(s_loc, axis=-1, keepdims=True)), sk)
        p_ctx = jnp.exp(s_ctx - m)
        p_loc = jnp.exp(s_loc - m)
        den = (jnp.sum(p_ctx, axis=-1, keepdims=True) + jnp.sum(p_loc, axis=-1, keepdims=True)
               + jnp.exp(sk - m))
        o = (jnp.dot(p_ctx.astype(BF16), vc[:, g * d:(g + 1) * d], preferred_element_type=F32)
             + jnp.dot(p_loc.astype(BF16), vw[:, g * d:(g + 1) * d], preferred_element_type=F32))
        outs.append(o / den)
    o_ref[...] = jnp.concatenate(outs, axis=-1).astype(o_ref.dtype)


def _attn_lat_b(z, sink, cache_k, cache_v, tables_q, tables_k):
    nb = DEC_SEQ // BAND_BLOCK
    qrow0 = T_PROMPT // BAND_BLOCK
    srow0 = T_PROMPT // DEC_SEQ
    qc = 3 * CONV_DIM // 512
    kc = (3 * CONV_DIM + N_HEADS_B * HEAD_DIM_B) // 256
    kvw = N_KV_B * HEAD_DIM_B
    return pl.pallas_call(
        _attn_lat_b_kernel, grid=(DEC_BATCH, nb),
        in_specs=[pl.BlockSpec(memory_space=pltpu.SMEM),
                  pl.BlockSpec((BAND_BLOCK, 512), lambda b, n: (qrow0 + b * nb + n, qc)),
                  pl.BlockSpec((DEC_SEQ, 256), lambda b, n: (srow0 + b, kc)),
                  pl.BlockSpec((None, PAST_LEN, kvw), lambda b, n: (b, 0, 0)),
                  pl.BlockSpec((None, PAST_LEN, kvw), lambda b, n: (b, 0, 0)),
                  pl.BlockSpec((BAND_BLOCK, 512), lambda b, n: (n, 0)),
                  pl.BlockSpec((BAND_BLOCK, 512), lambda b, n: (n, 0)),
                  pl.BlockSpec((DEC_SEQ, kvw), lambda b, n: (0, 0)),
                  pl.BlockSpec((DEC_SEQ, kvw), lambda b, n: (0, 0))],
        out_specs=pl.BlockSpec((BAND_BLOCK, 512), lambda b, n: (b * nb + n, 0)),
        out_shape=jax.ShapeDtypeStruct((T_SAMPLE, 512), BF16),
        compiler_params=_params(("arbitrary", "arbitrary")), name="attn_lat_b",
    )(sink, z, z, cache_k, cache_v, tables_q[0], tables_q[1], tables_k[0], tables_k[1])


def _rms(x, g):
    ms = jnp.mean(x * x, axis=-1, keepdims=True)
    return x * lax.rsqrt(ms + RMS_EPS) * g


def _odd_prep_kernel(z_ref, gq_ref, wuq_ref, gkv_ref, wukv_ref, gsgu_ref, ws_ref, bst_ref,
                     q_ref, ckv_ref, kv_ref, yd_ref, *, rows):
    z = z_ref[...]
    cq = z[:, :Q_LORA]
    ckv = z[:, Q_LORA:Q_LORA + KV_LORA]
    dz = z[:, Q_LORA + KV_LORA:KR_BLOCK]
    q_ref[...] = jnp.dot(_rms(cq, gq_ref[...]).astype(BF16), wuq_ref[...],
                         preferred_element_type=F32).astype(q_ref.dtype)
    ckv_n = _rms(ckv, gkv_ref[...])
    ckv_ref[...] = ckv_n
    kv_ref[...] = jnp.dot(ckv_n.astype(BF16), wukv_ref[...],
                          preferred_element_type=F32).astype(kv_ref.dtype)
    a = jax.nn.gelu(dz)
    u = a[:, :D_D]
    vv = _rms(a[:, D_D:], gsgu_ref[...]).astype(BF16)
    gd = D_D // N_GROUPS_D
    chunks = []
    for c in range(rows // CHUNK):
        groups = []
        for g in range(N_GROUPS_D):
            blk = vv[c * CHUNK:(c + 1) * CHUNK, g * gd:(g + 1) * gd]
            mixed = jnp.dot(ws_ref[g], blk, preferred_element_type=F32) + bst_ref[:, g:g + 1]
            groups.append(mixed)
        chunks.append(jnp.concatenate(groups, axis=-1))
    mixed = jnp.concatenate(chunks, axis=0)
    yd_ref[...] = (u * mixed).astype(yd_ref.dtype)


def _odd_prep(z, g_q, w_uq, g_kv, w_ukv, g_sgu, w_s, b_st):
    rows = SEQ
    full = lambda shape: pl.BlockSpec(shape, lambda i: tuple(0 for _ in shape))
    nq = w_uq.shape[1]
    nkv = w_ukv.shape[1]
    return pl.pallas_call(
        functools.partial(_odd_prep_kernel, rows=rows), grid=(T_ALL // rows,),
        in_specs=[pl.BlockSpec((rows, ODD_IN_PAD), lambda i: (i, 0)),
                  full((1, Q_LORA)), full((Q_LORA, nq)), full((1, KV_LORA)), full((KV_LORA, nkv)),
                  full((1, D_D)), full((N_GROUPS_D, CHUNK, CHUNK)), full((CHUNK, N_GROUPS_D))],
        out_specs=[pl.BlockSpec((rows, nq), lambda i: (i, 0)),
                   pl.BlockSpec((rows, KV_LORA), lambda i: (i, 0)),
                   pl.BlockSpec((rows, nkv), lambda i: (i, 0)),
                   pl.BlockSpec((rows, D_D), lambda i: (i, 0))],
        out_shape=[jax.ShapeDtypeStruct((T_ALL, nq), F32),
                   jax.ShapeDtypeStruct((T_ALL, KV_LORA), F32),
                   jax.ShapeDtypeStruct((T_ALL, nkv), BF16),
                   jax.ShapeDtypeStruct((T_ALL, D_D), BF16)],
        compiler_params=_params(("arbitrary",), VMEM_LIMIT), name="odd_prep",
    )(z, g_q, w_uq, g_kv, w_ukv, g_sgu, w_s, b_st)


def _attn_ctx_c_kernel(q_ref, kv_ref, kr_ref, o_ref):
    nk = N_HEADS_C * MLA_HEAD
    q = q_ref[...].astype(BF16)
    kv = kv_ref[...]
    kr = kr_ref[...].astype(BF16)
    outs = []
    for h in range(N_HEADS_C):
        k = kv[:, h * MLA_HEAD:(h + 1) * MLA_HEAD] + kr
        v = kv[:, nk + h * V_DIM_C:nk + (h + 1) * V_DIM_C]
        s = _nt_dot(q[:, h * MLA_HEAD:(h + 1) * MLA_HEAD], k) * MLA_SCALE
        m = jnp.max(s, axis=-1, keepdims=True)
        p = jnp.exp(s - m)
        den = jnp.sum(p, axis=-1, keepdims=True)
        outs.append(jnp.dot(p.astype(BF16), v, preferred_element_type=F32) / den)
    o_ref[...] = jnp.concatenate(outs, axis=-1).astype(o_ref.dtype)


def _attn_ctx_c(q, kv, z):
    return pl.pallas_call(
        _attn_ctx_c_kernel, grid=(BATCH,),
        in_specs=[pl.BlockSpec((SEQ, q.shape[1]), lambda b: (b, 0)),
                  pl.BlockSpec((SEQ, kv.shape[1]), lambda b: (b, 0)),
                  pl.BlockSpec((SEQ, LANES), lambda b: (b, KR_BLOCK // LANES))],
        out_specs=pl.BlockSpec((SEQ, 512), lambda b: (b, 0)),
        out_shape=jax.ShapeDtypeStruct((T_PROMPT, 512), BF16),
        compiler_params=_params(("arbitrary",)), name="attn_ctx_c",
    )(q, kv, z)


MLA_Q_BLOCK = 256


def _attn_lat_c_kernel(q_ref, kv_ref, kr_ref, kvc_ref, krc_ref, cq_ref, sq_ref, ck_ref, sk_ref, o_ref):
    nk = N_HEADS_C * MLA_HEAD
    q = _rope(q_ref[...], cq_ref[...], sq_ref[...], ROPE_C).astype(BF16)
    kr_l = _rope(kr_ref[...], ck_ref[...], sk_ref[...], ROPE_C).astype(BF16)
    kr_c = krc_ref[...].astype(BF16)
    kv_l = kv_ref[...]
    kv_c = kvc_ref[...]
    outs = []
    for h in range(N_HEADS_C):
        qh = q[:, h * MLA_HEAD:(h + 1) * MLA_HEAD]
        ks = slice(h * MLA_HEAD, (h + 1) * MLA_HEAD)
        vs = slice(nk + h * V_DIM_C, nk + (h + 1) * V_DIM_C)
        s_c = _nt_dot(qh, kv_c[:, ks] + kr_c) * MLA_SCALE
        s_l = _nt_dot(qh, kv_l[:, ks] + kr_l) * MLA_SCALE
        m = jnp.maximum(jnp.max(s_c, axis=-1, keepdims=True), jnp.max(s_l, axis=-1, keepdims=True))
        p_c = jnp.exp(s_c - m)
        p_l = jnp.exp(s_l - m)
        den = jnp.sum(p_c, axis=-1, keepdims=True) + jnp.sum(p_l, axis=-1, keepdims=True)
        o = (jnp.dot(p_c.astype(BF16), kv_c[:, vs], preferred_element_type=F32)
             + jnp.dot(p_l.astype(BF16), kv_l[:, vs], preferred_element_type=F32))
        outs.append(o / den)
    o_ref[...] = jnp.concatenate(outs, axis=-1).astype(o_ref.dtype)


def _attn_lat_c(q, kv, z, kv_cache, kr_cache, tables_q, tables_k):
    tq = MLA_Q_BLOCK
    nb = DEC_SEQ // tq
    qrow0 = T_PROMPT // tq
    srow0 = T_PROMPT // DEC_SEQ
    nr = q.shape[1]
    return pl.pallas_call(
        _attn_lat_c_kernel, grid=(DEC_BATCH, nb),
        in_specs=[pl.BlockSpec((tq, q.shape[1]), lambda b, n: (qrow0 + b * nb + n, 0)),
                  pl.BlockSpec((DEC_SEQ, kv.shape[1]), lambda b, n: (srow0 + b, 0)),
                  pl.BlockSpec((DEC_SEQ, LANES), lambda b, n: (srow0 + b, KR_BLOCK // LANES)),
                  pl.BlockSpec((None, PAST_LEN, kv.shape[1]), lambda b, n: (b, 0, 0)),
                  pl.BlockSpec((None, PAST_LEN, LANES), lambda b, n: (b, 0, 0)),
                  pl.BlockSpec((tq, nr), lambda b, n: (n, 0)),
                  pl.BlockSpec((tq, nr), lambda b, n: (n, 0)),
                  pl.BlockSpec((DEC_SEQ, LANES), lambda b, n: (0, 0)),
                  pl.BlockSpec((DEC_SEQ, LANES), lambda b, n: (0, 0))],
        out_specs=pl.BlockSpec((tq, 512), lambda b, n: (b * nb + n, 0)),
        out_shape=jax.ShapeDtypeStruct((T_SAMPLE, 512), BF16),
        compiler_params=_params(("arbitrary", "arbitrary"), VMEM_LIMIT), name="attn_lat_c",
    )(q, kv, z, kv_cache, kr_cache, tables_q[0], tables_q[1], tables_k[0], tables_k[1])


def _top16(s, tie_safe, want_rank):
    kio = lax.broadcasted_iota(jnp.int32, s.shape, 0).astype(F32)
    rio = lax.broadcasted_iota(jnp.int32, (PEER_TOPK, s.shape[1]), 0)
    rank = jnp.full(s.shape, float(PEER_TOPK), F32)
    vals = jnp.zeros((PEER_TOPK, s.shape[1]), F32)
    x = s
    for it in range(PEER_TOPK):
        m = jnp.max(x, axis=0, keepdims=True)
        hit = x == m
        if tie_safe:
            first = jnp.min(jnp.where(hit, kio, float(N_KEYS)), axis=0, keepdims=True)
            hit = kio == first
        if want_rank:
            rank = jnp.where(hit, float(it), rank)
        x = jnp.where(hit, -jnp.inf, x)
        vals = jnp.where(rio == it, m, vals)
    removed = jnp.sum(jnp.where(x == -jnp.inf, 1.0, 0.0), axis=0, keepdims=True)
    return (rank if want_rank else None), vals, removed


def _pair_counts(sv1, sv2, tie_safe):
    aio = lax.broadcasted_iota(jnp.int32, sv1.shape, 0).astype(F32)
    cnt = jnp.zeros(sv1.shape, F32)
    front = sv1 + sv2[0:1, :]
    top = front[0:1, :]
    z = jnp.zeros_like(top)
    for _ in range(PEER_TOPK):
        m = jnp.max(front, axis=0, keepdims=True)
        hit = front == m
        if tie_safe:
            first = jnp.min(jnp.where(hit, aio, float(PEER_TOPK)), axis=0, keepdims=True)
            hit = aio == first
        z = z + jnp.exp(m - top)
        cnt = jnp.where(hit, cnt + 1.0, cnt)
        nxt = jnp.full(sv1.shape, -jnp.inf, F32)
        for b in range(1, PEER_TOPK):
            nxt = jnp.where(cnt == float(b), sv2[b:b + 1, :], nxt)
        front = jnp.where(hit, sv1 + nxt, front)
    return cnt, z


def _peer_select_kernel(q_ref, keys_ref, r2_ref, e2_ref, cnt_ref, e1g_ref):
    def head(h, carry):
        s1 = _nt_dot(keys_ref[2 * h], q_ref[2 * h].astype(BF16))
        s2 = _nt_dot(keys_ref[2 * h + 1], q_ref[2 * h + 1].astype(BF16))

        def emit(rank2, sv1, sv2, cnt_keys, z):
            r2_ref[h] = rank2.astype(r2_ref.dtype)
            e2_ref[h] = jnp.exp(s2 - sv2[0:1, :]).astype(e2_ref.dtype)
            cnt_ref[h] = cnt_keys
            e1g_ref[h] = jnp.exp(s1 - sv1[0:1, :]) / z

        _, sv1, removed1 = _top16(s1, False, False)
        rank2, sv2, removed2 = _top16(s2, False, True)
        cnt, z = _pair_counts(sv1, sv2, False)
        cnt_keys = jnp.zeros_like(s1)
        for a in range(PEER_TOPK):
            cnt_keys = jnp.where(s1 == sv1[a:a + 1, :], cnt[a:a + 1, :], cnt_keys)
        emit(rank2, sv1, sv2, cnt_keys, z)
        taken = jnp.sum(cnt, axis=0, keepdims=True)
        k = float(PEER_TOPK)
        tied = jnp.where((removed1 != k) | (removed2 != k) | (taken != k), 1.0, 0.0)

        @pl.when(jnp.max(tied) > 0.0)
        def _():
            rank1, sv1, _ = _top16(s1, True, True)
            rank2, sv2, _ = _top16(s2, True, True)
            cnt, z = _pair_counts(sv1, sv2, True)
            cnt_keys = jnp.zeros_like(s1)
            for a in range(PEER_TOPK):
                cnt_keys = jnp.where(rank1 == float(a), cnt[a:a + 1, :], cnt_keys)
            emit(rank2, sv1, sv2, cnt_keys, z)
        return carry
    lax.fori_loop(0, PEER_HEADS, head, 0)


def _peer_select(qp, subkeys):
    shp = jax.ShapeDtypeStruct((PEER_HEADS, N_KEYS, T_ALL), F32)
    shp_b = jax.ShapeDtypeStruct((PEER_HEADS, N_KEYS, T_ALL), BF16)
    ospec = pl.BlockSpec((PEER_HEADS, N_KEYS, SEL_BLOCK), lambda i: (0, 0, i))
    return pl.pallas_call(
        _peer_select_kernel, grid=(T_ALL // SEL_BLOCK,),
        in_specs=[pl.BlockSpec((2 * PEER_HEADS, SEL_BLOCK, N_KEYS), lambda i: (0, i, 0)),
                  pl.BlockSpec(subkeys.shape, lambda i: (0, 0, 0))],
        out_specs=[ospec, ospec, ospec, ospec],
        out_shape=[shp_b, shp_b, shp, shp],
        compiler_params=_params(("arbitrary",)), name="peer_select",
    )(qp, subkeys)


def _peer_dense_kernel(ht_ref, u_ref, vt_ref, r2_ref, e2_ref, cnt_ref, e1g_ref, x_ref, gate_ref,
                       o_ref, acc_ref, p_ref):
    j = pl.program_id(1)

    @pl.when(j == 0)
    def _():
        acc_ref[...] = jnp.zeros_like(acc_ref)

    ht = ht_ref[...]
    for a in range(PEER_I1):
        act = jnp.dot(u_ref[a * N_KEYS:(a + 1) * N_KEYS, :], ht, preferred_element_type=F32)
        w = jnp.zeros(act.shape, BF16)
        for h in range(PEER_HEADS):
            cnt = jnp.broadcast_to(cnt_ref[h, a:a + 1, :], act.shape).astype(BF16)
            e1g = jnp.broadcast_to(e1g_ref[h, a:a + 1, :], act.shape).astype(BF16)
            w = w + jnp.where(r2_ref[h] < cnt, e2_ref[h] * e1g, jnp.zeros((), BF16))
        p_ref[a * N_KEYS:(a + 1) * N_KEYS, :] = w * jax.nn.gelu(act).astype(BF16)
    acc_ref[...] += jnp.dot(vt_ref[...], p_ref[...], preferred_element_type=F32)

    @pl.when(j == pl.num_programs(1) - 1)
    def _():
        o_ref[...] = x_ref[...] + gate_ref[...] * acc_ref[...].T


def _peer_dense(ht, u_bf, vt_bf, r2, e2, cnt, e1g, x, gate):
    tb = PEER_TOKENS
    et = PEER_I1 * N_KEYS
    grp = lambda i, j: (_group_of_block(i, tb), 0, 0)
    tok = pl.BlockSpec((PEER_HEADS, N_KEYS, tb), lambda i, j: (0, 0, i))
    key1 = pl.BlockSpec((PEER_HEADS, PEER_I1, tb), lambda i, j: (0, j, i))
    return pl.pallas_call(
        _peer_dense_kernel, grid=(T_ALL // tb, N_EXPERTS // et),
        in_specs=[pl.BlockSpec((D_MODEL, tb), lambda i, j: (0, i)),
                  pl.BlockSpec((et, D_MODEL), lambda i, j: (j, 0)),
                  pl.BlockSpec((D_MODEL, et), lambda i, j: (0, j)),
                  tok, tok, key1, key1,
                  pl.BlockSpec((tb, D_MODEL), lambda i, j: (i, 0)),
                  pl.BlockSpec((None, 1, D_MODEL), grp)],
        out_specs=pl.BlockSpec((tb, D_MODEL), lambda i, j: (i, 0)),
        out_shape=jax.ShapeDtypeStruct((T_ALL, D_MODEL), F32),
        scratch_shapes=[pltpu.VMEM((D_MODEL, tb), F32), pltpu.VMEM((et, tb), BF16)],
        compiler_params=_params(("arbitrary", "arbitrary"), VMEM_LIMIT), name="peer_dense",
    )(ht, u_bf, vt_bf, r2, e2, cnt, e1g, x, gate)


def _peer_layer(x, g2, shift, scale, gate, wq_bf, subkeys_bf, u_bf, vt_bf):
    qp, ht = _ln_proj(x, g2, shift, scale, wq_bf, with_ht=True)
    r2, e2, cnt, e1g = _peer_select(qp, subkeys_bf)
    return _peer_dense(ht, u_bf, vt_bf, r2, e2, cnt, e1g, x, gate)


def kernel(x_prompt, x_sample, cache_attn_k, cache_attn_v, cache_mla_ckv, cache_mla_krope, c, c_ctx, w_mod, b_mod, g_norm1, g_norm2, ev_w_in, ev_conv_w, ev_sink, ev_w_out, od_w_in, od_g_qnorm, od_w_uq, od_g_kvnorm, od_w_ukv, od_g_sgu, od_w_spatial, od_b_spatial, od_w_out, peer_wq, peer_subkeys, peer_u, peer_v, g_final):
    x = jnp.concatenate([x_prompt.reshape(T_PROMPT, D_MODEL), x_sample.reshape(T_SAMPLE, D_MODEL)], axis=0)
    cvec = jnp.concatenate([c_ctx[None, :], c, jnp.zeros((8 - N_GROUPS_MOD, D_MODEL), F32)], axis=0)

    def mods(layer):
        m = _modulation(cvec, w_mod[layer], b_mod[layer])[:N_GROUPS_MOD]
        m = m.reshape(N_GROUPS_MOD, N_MOD, 1, D_MODEL)
        return [m[:, k] for k in range(N_MOD)]

    def peer(layer, x, m):
        keys = peer_subkeys[layer].reshape(PEER_HEADS * 2, N_KEYS, N_KEYS).astype(BF16)
        return _peer_layer(x, g_norm2[layer][None, :], m[3], m[4], m[5],
                           peer_wq[layer].astype(BF16), keys,
                           peer_u[layer].astype(BF16), peer_v[layer].T.astype(BF16))

    m = mods(0)
    z0 = _ln_proj(x, g_norm1[0][None, :], m[0], m[1], ev_w_in[0].astype(BF16))
    y_a = _gated_conv(z0, ev_conv_w[0])
    kvw = N_KV_B * HEAD_DIM_B
    yb_p = _attn_ctx_b(z0, ev_sink[0])
    yb_s = _attn_lat_b(z0, ev_sink[0],
                       cache_attn_k[:, 0].reshape(DEC_BATCH, PAST_LEN, kvw),
                       cache_attn_v[:, 0].reshape(DEC_BATCH, PAST_LEN, kvw),
                       _rope_tables(HEAD_DIM_B, N_HEADS_B), _rope_tables(HEAD_DIM_B, N_KV_B))
    y_b = jnp.concatenate([yb_p, yb_s], axis=0)
    w_out = ev_w_out[0].astype(BF16)
    x = _out_proj(y_a, y_b, w_out[:CONV_DIM], w_out[CONV_DIM:], x, m[2])
    x = peer(0, x, m)
    kcol = 3 * CONV_DIM + N_HEADS_B * HEAD_DIM_B
    new_attn_k = z0[:T_PROMPT, kcol:kcol + kvw].reshape(BATCH, 1, SEQ, N_KV_B, HEAD_DIM_B)
    new_attn_v = z0[:T_PROMPT, kcol + kvw:kcol + 2 * kvw].reshape(BATCH, 1, SEQ, N_KV_B, HEAD_DIM_B)

    m = mods(1)
    w_in = od_w_in[0]
    w_in = jnp.concatenate([w_in[:, :Q_LORA + KV_LORA], w_in[:, Q_LORA + KV_LORA + ROPE_C:],
                            jnp.zeros((D_MODEL, KR_LANE), F32),
                            w_in[:, Q_LORA + KV_LORA:Q_LORA + KV_LORA + ROPE_C],
                            jnp.zeros((D_MODEL, ODD_IN_PAD - KR_COL - ROPE_C), F32)], axis=1)
    z1 = _ln_proj(x, g_norm1[1][None, :], m[0], m[1], w_in.astype(BF16))
    w_uq = od_w_uq[0].reshape(Q_LORA, N_HEADS_C, NOPE_C + ROPE_C)
    w_uq = jnp.pad(w_uq, ((0, 0), (0, 0), (0, MLA_HEAD - NOPE_C - ROPE_C)))
    w_uq = w_uq.reshape(Q_LORA, N_HEADS_C * MLA_HEAD).astype(BF16)
    w_ukv = od_w_ukv[0].reshape(KV_LORA, N_HEADS_C, NOPE_C + V_DIM_C)
    w_uk = jnp.pad(w_ukv[:, :, :NOPE_C], ((0, 0), (0, 0), (0, MLA_HEAD - NOPE_C)))
    w_ukv = jnp.concatenate([w_uk.reshape(KV_LORA, -1), w_ukv[:, :, NOPE_C:].reshape(KV_LORA, -1)],
                            axis=1).astype(BF16)
    q, ckv_n, kv, y_d = _odd_prep(z1, od_g_qnorm[0][None, :], w_uq, od_g_kvnorm[0][None, :], w_ukv,
                                  od_g_sgu[0][None, :], od_w_spatial[0].astype(BF16), od_b_spatial[0].T)
    kv_cache = _matmul(cache_mla_ckv[:, 0].reshape(DEC_BATCH * PAST_LEN, KV_LORA), w_ukv, BF16, PAST_LEN)
    kr_cache = jnp.pad(cache_mla_krope[:, 0], ((0, 0), (0, 0), (KR_LANE, LANES - KR_LANE - ROPE_C)))
    yc_p = _attn_ctx_c(q, kv, z1)
    yc_s = _attn_lat_c(q, kv, z1, kv_cache.reshape(DEC_BATCH, PAST_LEN, -1), kr_cache,
                       _mla_q_tables(), _rope_tables(ROPE_C, LANES // ROPE_C))
    y_c = jnp.concatenate([yc_p, yc_s], axis=0)
    w_out = od_w_out[0].astype(BF16)
    x = _out_proj(y_c, y_d, w_out[:N_HEADS_C * V_DIM_C], w_out[N_HEADS_C * V_DIM_C:], x, m[2])
    x = peer(1, x, m)
    new_mla_ckv = ckv_n[:T_PROMPT].reshape(BATCH, 1, SEQ, KV_LORA)
    new_mla_krope = z1[:T_PROMPT, KR_COL:KR_COL + ROPE_C].reshape(BATCH, 1, SEQ, ROPE_C)

    y = _final_norm(x, g_final[None, :])
    y_prompt = y[:T_PROMPT].reshape(BATCH, SEQ, D_MODEL)
    y_sample = y[T_PROMPT:].reshape(DEC_BATCH, DEC_SEQ, D_MODEL)
    return (y_prompt, y_sample, new_attn_k, new_attn_v, new_mla_ckv, new_mla_krope)
```
